```python
import jax, jax.numpy as jnp
from jax import lax
import numpy as np

D_MODEL = 1024
BATCH = 8
SEQ = 4096
DEPTH = 2

N_A_LAYERS = DEPTH // 2
N_B_LAYERS = DEPTH - N_A_LAYERS
N_SUBLAYERS = 3
GLA_HEADS = 4
GLA_DK = D_MODEL // 2 // GLA_HEADS
GLA_DV = D_MODEL // GLA_HEADS
GLA_GATE_RANK = 16
GLA_TAU = 16.0
GLA_CHUNK = 64
MLA_HEADS = 8
QK_NOPE = 128
QK_ROPE = 64
V_HEAD = 128
KV_LORA = 256
Q_LORA = 384
ROPE_THETA = 10000.0
Q_BLOCK = 128
D_FF = 2816
EPS = 1e-6
MAX_POS_OFFSET = 1024

kernel_name = 'gla_mla_yoco_hybrid'


def rmsnorm(x, g):
    xf = x.astype(jnp.float32)
    y = xf * lax.rsqrt(jnp.mean(xf * xf, axis=-1, keepdims=True) + EPS)
    return (y * g.astype(jnp.float32)).astype(x.dtype)


def swiglu(h, w_gu, w_down):
    gate, up = jnp.split(h @ w_gu, 2, axis=-1)
    return (jax.nn.silu(gate) * up) @ w_down


def sublayer(x, shift, scale, gate, g_pre, g_post, fn, res_weight):
    h = rmsnorm(x, g_pre) * (1 + scale[:, None, :]) + shift[:, None, :]
    return x + res_weight * gate[:, None, :] * rmsnorm(fn(h), g_post)


def rope_tables(positions):
    inv_freq = ROPE_THETA ** (-jnp.arange(0, QK_ROPE, 2, dtype=jnp.float32) / QK_ROPE)
    ang = positions.astype(jnp.float32)[..., None] * inv_freq
    return jnp.cos(ang), jnp.sin(ang)


def apply_rope(x, cos, sin):
    cos = cos.astype(x.dtype)
    sin = sin.astype(x.dtype)
    x1, x2 = jnp.split(x, 2, axis=-1)
    return jnp.concatenate([x1 * cos - x2 * sin, x1 * sin + x2 * cos], axis=-1)


def gla_chunked(q, k, v, log_a):
    B, S, H, DK = q.shape
    C = GLA_CHUNK
    N = S // C
    f32 = jnp.float32

    def chunks(t):
        return t.astype(f32).reshape(B, N, C, H, t.shape[-1]).transpose(0, 3, 1, 2, 4)

    q, k, v, log_a = chunks(q) * DK ** -0.5, chunks(k), chunks(v), chunks(log_a)
    b = jnp.cumsum(log_a, axis=3)
    b_last = b[:, :, :, -1:, :]
    q_dec = q * jnp.exp(b)
    k_dec = k * jnp.exp(-b)
    causal = jnp.tril(jnp.ones((C, C), dtype=bool))
    att = jnp.where(causal, jnp.einsum('bhnid,bhnjd->bhnij', q_dec, k_dec), 0.0)
    o_intra = jnp.einsum('bhnij,bhnjv->bhniv', att, v)
    upd = jnp.einsum('bhncd,bhncv->bhndv', k * jnp.exp(b_last - b), v)
    decay = jnp.exp(b_last[:, :, :, 0, :])

    def step(state, inp):
        dec, u = inp
        return dec[..., None] * state + u, state

    s0 = jnp.zeros((B, H, DK, v.shape[-1]), f32)
    _, states = lax.scan(step, s0, (jnp.moveaxis(decay, 2, 0), jnp.moveaxis(upd, 2, 0)))
    states = jnp.moveaxis(states, 0, 2)
    o = o_intra + jnp.einsum('bhncd,bhndv->bhncv', q_dec, states)
    return o.transpose(0, 2, 3, 1, 4).reshape(B, S, H, v.shape[-1])


def gla_mixer(h, w_in, w_gate_up, b_gate, g_out, w_out):
    B, S, _ = h.shape
    kd = GLA_HEADS * GLA_DK
    vd = GLA_HEADS * GLA_DV
    q, k, v, g_low, r = jnp.split(h @ w_in, [kd, 2 * kd, 2 * kd + vd, 2 * kd + vd + GLA_GATE_RANK], axis=-1)
    log_a = jax.nn.log_sigmoid((g_low @ w_gate_up + b_gate).astype(jnp.float32)) / GLA_TAU
    o = gla_chunked(q.reshape(B, S, GLA_HEADS, GLA_DK), k.reshape(B, S, GLA_HEADS, GLA_DK),
                    v.reshape(B, S, GLA_HEADS, GLA_DV), log_a.reshape(B, S, GLA_HEADS, GLA_DK))
    o = rmsnorm(o, g_out).astype(h.dtype).reshape(B, S, vd)
    return (o * jax.nn.silu(r)) @ w_out


def mla_shared_kv(h, w_kv_a, g_kv, w_kv_b, cos, sin):
    B, S, _ = h.shape
    c_kv, k_pe = jnp.split(h @ w_kv_a, [KV_LORA], axis=-1)
    c_kv = rmsnorm(c_kv, g_kv)
    k_rope = apply_rope(k_pe, cos, sin)
    kv = (c_kv @ w_kv_b).reshape(B, S, MLA_HEADS, QK_NOPE + V_HEAD)
    k_nope, v = jnp.split(kv, [QK_NOPE], axis=-1)
    return k_nope, k_rope, v


def mla_mixer(h, k_nope, k_rope, v, w_dq, g_q, w_uq, w_out, cos, sin):
    B, S, _ = h.shape
    c_q = rmsnorm(h @ w_dq, g_q)
    q = (c_q @ w_uq).reshape(B, S, MLA_HEADS, QK_NOPE + QK_ROPE)
    q_nope, q_rope = jnp.split(q, [QK_NOPE], axis=-1)
    q_rope = apply_rope(q_rope, cos[:, :, None, :], sin[:, :, None, :])
    scale = (QK_NOPE + QK_ROPE) ** -0.5
    nb = S // Q_BLOCK
    qn_blocks = q_nope.reshape(B, nb, Q_BLOCK, MLA_HEADS, QK_NOPE).transpose(1, 0, 2, 3, 4)
    qr_blocks = q_rope.reshape(B, nb, Q_BLOCK, MLA_HEADS, QK_ROPE).transpose(1, 0, 2, 3, 4)
    key_pos = jnp.arange(S)
    neg = jnp.finfo(jnp.float32).min

    def attend(args):
        qn, qr, blk = args
        s = (jnp.einsum('bqhd,bkhd->bhqk', qn, k_nope) +
             jnp.einsum('bqhr,bkr->bhqk', qr, k_rope)).astype(jnp.float32) * scale
        q_pos = blk * Q_BLOCK + jnp.arange(Q_BLOCK)
        s = jnp.where(key_pos[None, :] <= q_pos[:, None], s, neg)
        p = jax.nn.softmax(s, axis=-1).astype(v.dtype)
        return jnp.einsum('bhqk,bkhv->bqhv', p, v)

    o = lax.map(attend, (qn_blocks, qr_blocks, jnp.arange(nb)))
    o = o.transpose(1, 0, 2, 3, 4).reshape(B, S, MLA_HEADS * V_HEAD)
    return o @ w_out


def setup_inputs(seed: int = 0) -> dict:
    key = jax.random.key(seed)
    ks = jax.random.split(key, 24)
    D = D_MODEL
    f32 = jnp.float32

    def w(k, shape, fan_in):
        return jax.random.normal(k, shape, f32) * fan_in ** -0.5

    def gain(k, shape):
        return 1.0 + 0.1 * jax.random.normal(k, shape, f32)

    def bias(k, shape):
        return 0.01 * jax.random.normal(k, shape, f32)

    gla_in_cols = 2 * GLA_HEADS * GLA_DK + 2 * GLA_HEADS * GLA_DV + GLA_GATE_RANK
    offsets = jax.random.randint(ks[2], (BATCH, 1), 0, MAX_POS_OFFSET, dtype=jnp.int32)
    return {
        'x': jax.random.normal(ks[0], (BATCH, SEQ, D), f32),
        'c': jax.random.normal(ks[1], (BATCH, D), f32),
        'positions': offsets + jnp.arange(SEQ, dtype=jnp.int32)[None, :],
        'cond_w': w(ks[3], (DEPTH, D, 3 * N_SUBLAYERS * D), D),
        'cond_b': bias(ks[4], (DEPTH, 3 * N_SUBLAYERS * D)),
        'norm_g': gain(ks[5], (DEPTH, N_SUBLAYERS, 2, D)),
        'ffn_w_gu': w(ks[6], (DEPTH, 2, D, 2 * D_FF), D),
        'ffn_w_down': w(ks[7], (DEPTH, 2, D_FF, D), D_FF),
        'gla_w_in': w(ks[8], (N_A_LAYERS, D, gla_in_cols), D),
        'gla_w_gate_up': w(ks[9], (N_A_LAYERS, GLA_GATE_RANK, GLA_HEADS * GLA_DK), GLA_GATE_RANK),
        'gla_b_gate': 0.1 * jax.random.normal(ks[10], (N_A_LAYERS, GLA_HEADS * GLA_DK), f32),
        'gla_g_out': gain(ks[11], (N_A_LAYERS, GLA_DV)),
        'gla_w_out': w(ks[12], (N_A_LAYERS, GLA_HEADS * GLA_DV, D), GLA_HEADS * GLA_DV),
        'kv_g_in': gain(ks[13], (D,)),
        'kv_cond_w': w(ks[14], (D, 2 * D), D),
        'kv_cond_b': bias(ks[15], (2 * D,)),
        'mla_w_kv_a': w(ks[16], (D, KV_LORA + QK_ROPE), D),
        'mla_g_kv': gain(ks[17], (KV_LORA,)),
        'mla_w_kv_b': w(ks[18], (KV_LORA, MLA_HEADS * (QK_NOPE + V_HEAD)), KV_LORA),
        'mla_w_dq': w(ks[19], (N_B_LAYERS, D, Q_LORA), D),
        'mla_g_q': gain(ks[20], (N_B_LAYERS, Q_LORA)),
        'mla_w_uq': w(ks[21], (N_B_LAYERS, Q_LORA, MLA_HEADS * (QK_NOPE + QK_ROPE)), Q_LORA),
        'mla_w_out': w(ks[22], (N_B_LAYERS, MLA_HEADS * V_HEAD, D), MLA_HEADS * V_HEAD),
    }


def reference(x, c, positions, cond_w, cond_b, norm_g, ffn_w_gu, ffn_w_down,
              gla_w_in, gla_w_gate_up, gla_b_gate, gla_g_out, gla_w_out,
              kv_g_in, kv_cond_w, kv_cond_b, mla_w_kv_a, mla_g_kv, mla_w_kv_b,
              mla_w_dq, mla_g_q, mla_w_uq, mla_w_out):
    cos, sin = rope_tables(positions)
    c_act = jax.nn.silu(c)
    k_nope = k_rope = v_shared = None
    for layer in range(DEPTH):
        mods = jnp.split(c_act @ cond_w[layer] + cond_b[layer], 3 * N_SUBLAYERS, axis=-1)
        g = norm_g[layer]
        x = sublayer(x, mods[0], mods[1], mods[2], g[0, 0], g[0, 1],
                     lambda h: swiglu(h, ffn_w_gu[layer, 0], ffn_w_down[layer, 0]), 0.5)
        if layer < N_A_LAYERS:
            i = layer
            x = sublayer(x, mods[3], mods[4], mods[5], g[1, 0], g[1, 1],
                         lambda h: gla_mixer(h, gla_w_in[i], gla_w_gate_up[i], gla_b_gate[i],
                                             gla_g_out[i], gla_w_out[i]), 1.0)
        else:
            i = layer - N_A_LAYERS
            x = sublayer(x, mods[3], mods[4], mods[5], g[1, 0], g[1, 1],
                         lambda h: mla_mixer(h, k_nope, k_rope, v_shared, mla_w_dq[i], mla_g_q[i],
                                             mla_w_uq[i], mla_w_out[i], cos, sin), 1.0)
        x = sublayer(x, mods[6], mods[7], mods[8], g[2, 0], g[2, 1],
                     lambda h: swiglu(h, ffn_w_gu[layer, 1], ffn_w_down[layer, 1]), 0.5)
        if layer == N_A_LAYERS - 1:
            kv_shift, kv_scale = jnp.split(c_act @ kv_cond_w + kv_cond_b, 2, axis=-1)
            h_kv = rmsnorm(x, kv_g_in) * (1 + kv_scale[:, None, :]) + kv_shift[:, None, :]
            k_nope, k_rope, v_shared = mla_shared_kv(h_kv, mla_w_kv_a, mla_g_kv, mla_w_kv_b, cos, sin)
    return x
```

```python
import functools

import jax
import jax.numpy as jnp
from jax import lax
from jax.experimental import pallas as pl
from jax.experimental.pallas import tpu as pltpu

F32 = jnp.float32
BF16 = jnp.bfloat16

EPS = 1e-6
N_SUBLAYERS = 3
GLA_HEADS = 4
GLA_GATE_RANK = 16
GLA_TAU = 16.0
GLA_CHUNK = 64
MLA_HEADS = 8
QK_NOPE = 128
QK_ROPE = 64
V_HEAD = 128
KV_LORA = 256
ROPE_THETA = 10000.0

LANES = 128
VMEM_LIMIT_BYTES = 56 * 1024 * 1024

TOKEN_TILE = 512
FF_CHUNK = 256
ATTN_TILE = 512


def _params(*sem):
    return pltpu.CompilerParams(dimension_semantics=sem,
                                vmem_limit_bytes=VMEM_LIMIT_BYTES)


def _rms(x, g):
    return x * lax.rsqrt(jnp.mean(x * x, axis=-1, keepdims=True) + EPS) * g


def _silu(x):
    return x * jax.nn.sigmoid(x)


def _modulated_norm(x, g, scale, shift):
    return _rms(x, g) * (1.0 + scale) + shift


def _const_spec(shape):
    nd = len(shape)
    return pl.BlockSpec(shape, lambda *_: (0,) * nd)


def _tok_spec(tile, width):
    return pl.BlockSpec((1, tile, width), lambda b, i: (b, i, 0))


def _batch_vec_spec(width):
    return pl.BlockSpec((1, 1, width), lambda b, i: (b, 0, 0))


def _cond_kernel(c_ref, w_ref, b_ref, o_ref):
    c_act = _silu(c_ref[...])
    o_ref[0] = jnp.dot(c_act, w_ref[0], preferred_element_type=F32) + b_ref[0]


def _cond(c, w, b, col_tile):
    n_layers, d, n = w.shape
    batch = c.shape[0]
    return pl.pallas_call(
        _cond_kernel,
        grid=(n_layers, n // col_tile),
        in_specs=[
            pl.BlockSpec((batch, d), lambda l, j: (0, 0)),
            pl.BlockSpec((1, d, col_tile), lambda l, j: (l, 0, j)),
            pl.BlockSpec((1, 1, col_tile), lambda l, j: (l, 0, j)),
        ],
        out_specs=pl.BlockSpec((1, batch, col_tile), lambda l, j: (l, 0, j)),
        out_shape=jax.ShapeDtypeStruct((n_layers, batch, n), F32),
        compiler_params=_params("parallel", "parallel"),
        name="cond",
    )(c, w, b.reshape(n_layers, 1, n))


def _ffn_kernel(x_ref, shift_ref, scale_ref, gate_ref, gpre_ref, gpost_ref,
                wg_ref, wu_ref, wd_ref, o_ref, act_ref, *, res_weight):
    x = x_ref[0]
    h = _modulated_norm(x, gpre_ref[...], scale_ref[0], shift_ref[0]).astype(BF16)
    d_ff = wg_ref.shape[1]
    for j in range(d_ff // FF_CHUNK):
        cols = slice(j * FF_CHUNK, (j + 1) * FF_CHUNK)
        g = jnp.dot(h, wg_ref[:, cols], preferred_element_type=F32)
        u = jnp.dot(h, wu_ref[:, cols], preferred_element_type=F32)
        act_ref[:, cols] = (_silu(g) * u).astype(BF16)
    y = jnp.dot(act_ref[...], wd_ref[...], preferred_element_type=F32)
    o_ref[0] = x + (res_weight * gate_ref[0]) * _rms(y, gpost_ref[...])


def _ffn(x, shift, scale, gate, g_pre, g_post, w_g, w_u, w_d, res_weight):
    batch, seq, d = x.shape
    d_ff = w_g.shape[1]
    tile = TOKEN_TILE
    return pl.pallas_call(
        functools.partial(_ffn_kernel, res_weight=res_weight),
        grid=(batch, seq // tile),
        in_specs=[
            _tok_spec(tile, d),
            _batch_vec_spec(d), _batch_vec_spec(d), _batch_vec_spec(d),
            _const_spec((1, d)), _const_spec((1, d)),
            _const_spec((d, d_ff)), _const_spec((d, d_ff)), _const_spec((d_ff, d)),
        ],
        out_specs=_tok_spec(tile, d),
        out_shape=jax.ShapeDtypeStruct(x.shape, x.dtype),
        scratch_shapes=[pltpu.VMEM((tile, d_ff), BF16)],
        compiler_params=_params("parallel", "parallel"),
        name="ffn",
    )(x, shift, scale, gate, g_pre, g_post, w_g, w_u, w_d)


def _gla_kernel(x_ref, shift_ref, scale_ref, gate_ref, gpre_ref, gpost_ref,
                wq_ref, wk_ref, wv_ref, wlow_ref, wr_ref, wup_ref, bgate_ref,
                gout_ref, wout_ref, o_ref,
                state_ref, qd_ref, kd_ref, ku_ref, v_ref, dec_ref, mix_ref):
    tile = x_ref.shape[1]
    n_chunks = tile // GLA_CHUNK
    kd_all = wq_ref.shape[1]
    dk = kd_all // GLA_HEADS
    dv = wv_ref.shape[1] // GLA_HEADS

    @pl.when(pl.program_id(1) == 0)
    def _():
        state_ref[...] = jnp.zeros_like(state_ref)

    x = x_ref[0]
    h = _modulated_norm(x, gpre_ref[...], scale_ref[0], shift_ref[0]).astype(BF16)
    q = jnp.dot(h, wq_ref[...], preferred_element_type=F32) * dk ** -0.5
    k = jnp.dot(h, wk_ref[...], preferred_element_type=F32)
    v_ref[...] = jnp.dot(h, wv_ref[...], preferred_element_type=F32).astype(BF16)
    low = jnp.dot(h, wlow_ref[...], preferred_element_type=F32).astype(BF16)
    z = jnp.dot(low, wup_ref[...], preferred_element_type=F32) + bgate_ref[...]
    log_a = (jnp.minimum(z, 0.0) - jnp.log(1.0 + jnp.exp(-jnp.abs(z)))) / GLA_TAU

    row = lax.broadcasted_iota(jnp.int32, (tile, tile), 0)
    col = lax.broadcasted_iota(jnp.int32, (tile, tile), 1)
    tri = jnp.where((col <= row) & (col // GLA_CHUNK == row // GLA_CHUNK),
                    1.0, 0.0).astype(BF16)
    la_hi = log_a.astype(BF16)
    la_lo = (log_a - la_hi.astype(F32)).astype(BF16)
    b = (jnp.dot(tri, la_hi, preferred_element_type=F32) +
         jnp.dot(tri, la_lo, preferred_element_type=F32))
    b3 = b.reshape(n_chunks, GLA_CHUNK, kd_all)
    b_last = b3[:, GLA_CHUNK - 1:, :]
    qd_ref[...] = (q * jnp.exp(b)).astype(BF16)
    kd_ref[...] = (k * jnp.exp(-b)).astype(BF16)
    ku_ref[...] = (k.reshape(n_chunks, GLA_CHUNK, kd_all) *
                   jnp.exp(b_last - b3)).reshape(tile, kd_all).astype(BF16)
    dec_ref[...] = jnp.exp(b_last)

    ri = lax.broadcasted_iota(jnp.int32, (GLA_CHUNK, GLA_CHUNK), 0)
    ci = lax.broadcasted_iota(jnp.int32, (GLA_CHUNK, GLA_CHUNK), 1)
    causal = ci <= ri

    def chunk_step(n, carry):
        rows = pl.ds(pl.multiple_of(n * GLA_CHUNK, GLA_CHUNK), GLA_CHUNK)
        dec = dec_ref[n]
        for hd in range(GLA_HEADS):
            kcols = slice(hd * dk, (hd + 1) * dk)
            vcols = slice(hd * dv, (hd + 1) * dv)
            qd = qd_ref[rows, kcols]
            kd = kd_ref[rows, kcols]
            ku = ku_ref[rows, kcols]
            vc = v_ref[rows, vcols]
            att = lax.dot_general(qd, kd, (((1,), (1,)), ((), ())),
                                  preferred_element_type=F32)
            att = jnp.where(causal, att, 0.0).astype(BF16)
            state_t = state_ref[hd]
            o = jnp.dot(att, vc, preferred_element_type=F32)
            o = o + lax.dot_general(qd, state_t.astype(BF16),
                                    (((1,), (1,)), ((), ())),
                                    preferred_element_type=F32)
            mix_ref[rows, vcols] = o
            upd_t = lax.dot_general(vc, ku, (((0,), (0,)), ((), ())),
                                    preferred_element_type=F32)
            state_ref[hd] = dec[:, kcols] * state_t + upd_t
        return carry

    lax.fori_loop(0, n_chunks, chunk_step, 0)

    r = jnp.dot(h, wr_ref[...], preferred_element_type=F32)
    for hd in range(GLA_HEADS):
        vcols = slice(hd * dv, (hd + 1) * dv)
        mix_ref[:, vcols] = _rms(mix_ref[:, vcols], gout_ref[...])
    y = (mix_ref[...] * _silu(r)).astype(BF16)
    y = jnp.dot(y, wout_ref[...], preferred_element_type=F32)
    o_ref[0] = x + gate_ref[0] * _rms(y, gpost_ref[...])


def _gla(x, shift, scale, gate, g_pre, g_post, w_q, w_k, w_v, w_low, w_r, w_up,
         b_gate, g_out, w_out):
    batch, seq, d = x.shape
    tile = TOKEN_TILE
    kd_all = w_q.shape[1]
    vd_all = w_v.shape[1]
    dk = kd_all // GLA_HEADS
    dv = vd_all // GLA_HEADS
    weights = (w_q, w_k, w_v, w_low, w_r, w_up, b_gate, g_out, w_out)
    return pl.pallas_call(
        _gla_kernel,
        grid=(batch, seq // tile),
        in_specs=[
            _tok_spec(tile, d),
            _batch_vec_spec(d), _batch_vec_spec(d), _batch_vec_spec(d),
            _const_spec((1, d)), _const_spec((1, d)),
        ] + [_const_spec(w.shape) for w in weights],
        out_specs=_tok_spec(tile, d),
        out_shape=jax.ShapeDtypeStruct(x.shape, x.dtype),
        scratch_shapes=[
            pltpu.VMEM((GLA_HEADS, dv, dk), F32),
            pltpu.VMEM((tile, kd_all), BF16),
            pltpu.VMEM((tile, kd_all), BF16),
            pltpu.VMEM((tile, kd_all), BF16),
            pltpu.VMEM((tile, vd_all), BF16),
            pltpu.VMEM((tile // GLA_CHUNK, 1, kd_all), F32),
            pltpu.VMEM((tile, vd_all), F32),
        ],
        compiler_params=_params("parallel", "arbitrary"),
        name="gla",
    )(x, shift, scale, gate, g_pre, g_post, *weights)


def _swap_halves(x):
    half = x.shape[-1] // 2
    return jnp.concatenate([x[:, half:], x[:, :half]], axis=-1)


def _rope(x, cos2, sin2):
    return x * cos2 + _swap_halves(x) * sin2


def _kv_kernel(x_ref, shift_ref, scale_ref, pos_ref, freq_ref, gin_ref,
               wc_ref, wpe_ref, gkv_ref, wbk_ref, wbv_ref,
               k_ref, v_ref, cos_ref, sin_ref):
    x = x_ref[0]
    h = _modulated_norm(x, gin_ref[...], scale_ref[0], shift_ref[0]).astype(BF16)
    c_kv = jnp.dot(h, wc_ref[...], preferred_element_type=F32)
    k_pe = jnp.dot(h, wpe_ref[...], preferred_element_type=F32)
    c_kv = _rms(c_kv, gkv_ref[...]).astype(BF16)
    k_nope = jnp.dot(c_kv, wbk_ref[...], preferred_element_type=F32)
    v = jnp.dot(c_kv, wbv_ref[...], preferred_element_type=F32)

    ang = pos_ref[0].astype(F32) * freq_ref[...]
    cos = jnp.cos(ang)
    sin = jnp.sin(ang)
    cos2 = jnp.concatenate([cos, cos], axis=-1)
    sin2 = jnp.concatenate([-sin, sin], axis=-1)
    cos_ref[0] = cos2
    sin_ref[0] = sin2
    k_rope = _rope(k_pe, cos2, sin2).astype(BF16)
    for hd in range(MLA_HEADS):
        k_ref[0, hd, :, :QK_NOPE] = k_nope[:, hd * QK_NOPE:(hd + 1) * QK_NOPE].astype(BF16)
        k_ref[0, hd, :, QK_NOPE:] = k_rope
        v_ref[0, hd] = v[:, hd * V_HEAD:(hd + 1) * V_HEAD].astype(BF16)


def _shared_kv(x, shift, scale, pos, inv_freq, g_in, w_c, w_pe, g_kv, w_bk, w_bv):
    batch, seq, d = x.shape
    tile = TOKEN_TILE
    qk = QK_NOPE + QK_ROPE
    weights = (g_in, w_c, w_pe, g_kv, w_bk, w_bv)
    return pl.pallas_call(
        _kv_kernel,
        grid=(batch, seq // tile),
        in_specs=[
            _tok_spec(tile, d), _batch_vec_spec(d), _batch_vec_spec(d),
            _tok_spec(tile, 1), _const_spec(inv_freq.shape),
        ] + [_const_spec(w.shape) for w in weights],
        out_specs=[
            pl.BlockSpec((1, MLA_HEADS, tile, qk), lambda b, i: (b, 0, i, 0)),
            pl.BlockSpec((1, MLA_HEADS, tile, V_HEAD), lambda b, i: (b, 0, i, 0)),
            _tok_spec(tile, QK_ROPE), _tok_spec(tile, QK_ROPE),
        ],
        out_shape=[
            jax.ShapeDtypeStruct((batch, MLA_HEADS, seq, qk), BF16),
            jax.ShapeDtypeStruct((batch, MLA_HEADS, seq, V_HEAD), BF16),
            jax.ShapeDtypeStruct((batch, seq, QK_ROPE), F32),
            jax.ShapeDtypeStruct((batch, seq, QK_ROPE), F32),
        ],
        compiler_params=_params("parallel", "parallel"),
        name="shared_kv",
    )(x, shift, scale, pos, inv_freq, *weights)


def _q_kernel(x_ref, shift_ref, scale_ref, cos_ref, sin_ref, gpre_ref,
              wdq_ref, gq_ref, wn_ref, wr_ref, q_ref):
    x = x_ref[0]
    h = _modulated_norm(x, gpre_ref[...], scale_ref[0], shift_ref[0]).astype(BF16)
    c_q = jnp.dot(h, wdq_ref[...], preferred_element_type=F32)
    c_q = _rms(c_q, gq_ref[...]).astype(BF16)
    sm_scale = (QK_NOPE + QK_ROPE) ** -0.5
    q_nope = jnp.dot(c_q, wn_ref[...], preferred_element_type=F32) * sm_scale
    q_rope = jnp.dot(c_q, wr_ref[...], preferred_element_type=F32) * sm_scale
    cos2 = cos_ref[0]
    sin2 = sin_ref[0]
    for hd in range(MLA_HEADS):
        q_ref[0, hd, :, :QK_NOPE] = q_nope[:, hd * QK_NOPE:(hd + 1) * QK_NOPE].astype(BF16)
        q_ref[0, hd, :, QK_NOPE:] = _rope(
            q_rope[:, hd * QK_ROPE:(hd + 1) * QK_ROPE], cos2, sin2).astype(BF16)


def _mla_q(x, shift, scale, cos2, sin2, g_pre, w_dq, g_q, w_n, w_r):
    batch, seq, d = x.shape
    tile = TOKEN_TILE
    qk = QK_NOPE + QK_ROPE
    weights = (g_pre, w_dq, g_q, w_n, w_r)
    return pl.pallas_call(
        _q_kernel,
        grid=(batch, seq // tile),
        in_specs=[
            _tok_spec(tile, d), _batch_vec_spec(d), _batch_vec_spec(d),
            _tok_spec(tile, QK_ROPE), _tok_spec(tile, QK_ROPE),
        ] + [_const_spec(w.shape) for w in weights],
        out_specs=pl.BlockSpec((1, MLA_HEADS, tile, qk), lambda b, i: (b, 0, i, 0)),
        out_shape=jax.ShapeDtypeStruct((batch, MLA_HEADS, seq, qk), BF16),
        compiler_params=_params("parallel", "parallel"),
        name="mla_q",
    )(x, shift, scale, cos2, sin2, *weights)


def _attn_kernel(q_ref, k_ref, v_ref, o_ref):
    tile = q_ref.shape[2]
    qi = pl.program_id(2)
    q = q_ref[0, 0]
    neg = jnp.finfo(F32).min

    def block(kj, carry, masked):
        m_prev, l_prev, acc = carry
        rows = pl.ds(pl.multiple_of(kj * tile, tile), tile)
        s = lax.dot_general(q, k_ref[0, 0, rows, :], (((1,), (1,)), ((), ())),
                            preferred_element_type=F32)
        if masked:
            ri = lax.broadcasted_iota(jnp.int32, (tile, tile), 0)
            ci = lax.broadcasted_iota(jnp.int32, (tile, tile), 1)
            s = jnp.where(ci <= ri, s, neg)
        m_new = jnp.maximum(m_prev, jnp.max(s, axis=-1, keepdims=True))
        alpha = jnp.exp(m_prev - m_new)
        p = jnp.exp(s - m_new)
        l_new = alpha * l_prev + jnp.sum(p, axis=-1, keepdims=True)
        acc = alpha * acc + jnp.dot(p.astype(BF16), v_ref[0, 0, rows, :],
                                    preferred_element_type=F32)
        return m_new, l_new, acc

    init = (jnp.full((tile, 1), -jnp.inf, F32), jnp.zeros((tile, 1), F32),
            jnp.zeros((tile, v_ref.shape[3]), F32))
    carry = lax.fori_loop(0, qi, lambda kj, c: block(kj, c, False), init)
    _, l_fin, acc = block(qi, carry, True)
    o_ref[0] = (acc / l_fin).astype(o_ref.dtype)


def _attention(q, k, v):
    batch, heads, seq, qk = q.shape
    dv = v.shape[3]
    tile = ATTN_TILE
    return pl.pallas_call(
        _attn_kernel,
        grid=(batch, heads, seq // tile),
        in_specs=[
            pl.BlockSpec((1, 1, tile, qk), lambda b, h, i: (b, h, i, 0)),
            pl.BlockSpec((1, 1, seq, qk), lambda b, h, i: (b, h, 0, 0)),
            pl.BlockSpec((1, 1, seq, dv), lambda b, h, i: (b, h, 0, 0)),
        ],
        out_specs=pl.BlockSpec((1, tile, dv), lambda b, h, i: (b, i, h)),
        out_shape=jax.ShapeDtypeStruct((batch, seq, heads * dv), BF16),
        compiler_params=_params("parallel", "parallel", "parallel"),
        name="mla_attention",
    )(q, k, v)


def _proj_out_kernel(x_ref, a_ref, gate_ref, gpost_ref, w_ref, o_ref):
    y = jnp.dot(a_ref[0], w_ref[...], preferred_element_type=F32)
    o_ref[0] = x_ref[0] + gate_ref[0] * _rms(y, gpost_ref[...])


def _proj_out(x, a, gate, g_post, w_out):
    batch, seq, d = x.shape
    tile = TOKEN_TILE
    return pl.pallas_call(
        _proj_out_kernel,
        grid=(batch, seq // tile),
        in_specs=[
            _tok_spec(tile, d), _tok_spec(tile, a.shape[2]), _batch_vec_spec(d),
            _const_spec((1, d)), _const_spec(w_out.shape),
        ],
        out_specs=_tok_spec(tile, d),
        out_shape=jax.ShapeDtypeStruct(x.shape, x.dtype),
        compiler_params=_params("parallel", "parallel"),
        name="mla_out",
    )(x, a, gate, g_post, w_out)


def kernel(x, c, positions, cond_w, cond_b, norm_g, ffn_w_gu, ffn_w_down, gla_w_in, gla_w_gate_up, gla_b_gate, gla_g_out, gla_w_out, kv_g_in, kv_cond_w, kv_cond_b, mla_w_kv_a, mla_g_kv, mla_w_kv_b, mla_w_dq, mla_g_q, mla_w_uq, mla_w_out):
    batch, seq, d = x.shape
    depth = cond_w.shape[0]
    n_a = gla_w_in.shape[0]
    d_ff = ffn_w_down.shape[2]
    kd = GLA_HEADS * (d // 2 // GLA_HEADS)
    vd = d

    n_mod = 3 * N_SUBLAYERS
    mods = _cond(c, cond_w, cond_b, cond_w.shape[2] // 4)
    mods = mods.reshape(depth, batch, n_mod, 1, d)
    kv_mods = _cond(c, kv_cond_w[None], kv_cond_b[None], kv_cond_w.shape[1] // 2)
    kv_mods = kv_mods.reshape(batch, 2, 1, d)

    def mod(layer, idx):
        return mods[layer, :, idx]

    def gain(layer, sub, which):
        return norm_g[layer, sub, which].reshape(1, d)

    def ffn(x, layer, half, sub):
        w_gu = ffn_w_gu[layer, half].astype(BF16)
        return _ffn(x, mod(layer, 3 * sub), mod(layer, 3 * sub + 1), mod(layer, 3 * sub + 2),
                    gain(layer, sub, 0), gain(layer, sub, 1),
                    w_gu[:, :d_ff], w_gu[:, d_ff:],
                    ffn_w_down[layer, half].astype(BF16), 0.5)

    inv_freq = (ROPE_THETA ** (-jnp.arange(0, QK_ROPE, 2, dtype=F32) / QK_ROPE)
                ).reshape(1, QK_ROPE // 2)
    pos = positions.reshape(batch, seq, 1)

    shared = None
    for layer in range(depth):
        x = ffn(x, layer, 0, 0)
        if layer < n_a:
            i = layer
            w_in = gla_w_in[i].astype(BF16)
            w_low = jnp.zeros((d, LANES), BF16).at[:, :GLA_GATE_RANK].set(
                w_in[:, 2 * kd + vd:2 * kd + vd + GLA_GATE_RANK])
            w_up = jnp.zeros((LANES, kd), BF16).at[:GLA_GATE_RANK].set(
                gla_w_gate_up[i].astype(BF16))
            x = _gla(x, mod(layer, 3), mod(layer, 4), mod(layer, 5),
                     gain(layer, 1, 0), gain(layer, 1, 1),
                     w_in[:, :kd], w_in[:, kd:2 * kd], w_in[:, 2 * kd:2 * kd + vd],
                     w_low, w_in[:, 2 * kd + vd + GLA_GATE_RANK:], w_up,
                     gla_b_gate[i].reshape(1, kd), gla_g_out[i].reshape(1, -1),
                     gla_w_out[i].astype(BF16))
        else:
            i = layer - n_a
            k_all, v_all, cos2, sin2 = shared
            w_uq = mla_w_uq[i].astype(BF16).reshape(-1, MLA_HEADS, QK_NOPE + QK_ROPE)
            q = _mla_q(x, mod(layer, 3), mod(layer, 4), cos2, sin2,
                       gain(layer, 1, 0), mla_w_dq[i].astype(BF16),
                       mla_g_q[i].reshape(1, -1),
                       w_uq[:, :, :QK_NOPE].reshape(-1, MLA_HEADS * QK_NOPE),
                       w_uq[:, :, QK_NOPE:].reshape(-1, MLA_HEADS * QK_ROPE))
            a = _attention(q, k_all, v_all)
            x = _proj_out(x, a, mod(layer, 5), gain(layer, 1, 1),
                          mla_w_out[i].astype(BF16))
        x = ffn(x, layer, 1, 2)
        if layer == n_a - 1:
            w_kv_a = mla_w_kv_a.astype(BF16)
            w_kv_b = mla_w_kv_b.astype(BF16).reshape(KV_LORA, MLA_HEADS, QK_NOPE + V_HEAD)
            shared = _shared_kv(
                x, kv_mods[:, 0], kv_mods[:, 1], pos, inv_freq,
                kv_g_in.reshape(1, d), w_kv_a[:, :KV_LORA], w_kv_a[:, KV_LORA:],
                mla_g_kv.reshape(1, KV_LORA),
                w_kv_b[:, :, :QK_NOPE].reshape(KV_LORA, MLA_HEADS * QK_NOPE),
                w_kv_b[:, :, QK_NOPE:].reshape(KV_LORA, MLA_HEADS * V_HEAD))
    return x
```

```python
import functools

import jax
import jax.numpy as jnp
from jax import lax
from jax.experimental import pallas as pl
from jax.experimental.pallas import tpu as pltpu

F32 = jnp.float32
BF16 = jnp.bfloat16

EPS = 1e-6
N_SUBLAYERS = 3
GLA_HEADS = 4
GLA_GATE_RANK = 16
GLA_TAU = 16.0
GLA_CHUNK = 64
MLA_HEADS = 8
QK_NOPE = 128
QK_ROPE = 64
V_HEAD = 128
KV_LORA = 256
ROPE_THETA = 10000.0

LANES = 128
VMEM_LIMIT_BYTES = 56 * 1024 * 1024

TOKEN_TILE = 512
FFN_TILE = 1024
FFN_ROW_SPLIT = 2
FF_CHUNK = 256
ATTN_TILE = 512
ATTN_KEY_TILE = 256
ATTN_HEAD_GROUP = 4
ATTN_LOOKAHEAD = 2
LOG2E = 1.4426950408889634


def _params(*sem):
    return pltpu.CompilerParams(dimension_semantics=sem,
                                vmem_limit_bytes=VMEM_LIMIT_BYTES)


def _rms(x, g):
    return x * lax.rsqrt(jnp.mean(x * x, axis=-1, keepdims=True) + EPS) * g


def _silu(x):
    return x * jax.nn.sigmoid(x)


def _modulated_norm(x, g, scale, shift):
    return _rms(x, g) * (1.0 + scale) + shift


def _const_spec(shape):
    nd = len(shape)
    return pl.BlockSpec(shape, lambda *_: (0,) * nd)


def _tok_spec(tile, width):
    return pl.BlockSpec((1, tile, width), lambda b, i: (b, i, 0))


def _batch_vec_spec(width):
    return pl.BlockSpec((1, 1, width), lambda b, i: (b, 0, 0))


def _cond_kernel(c_ref, w_ref, b_ref, o_ref):
    c_act = _silu(c_ref[...])
    o_ref[0] = jnp.dot(c_act, w_ref[0], preferred_element_type=F32) + b_ref[0]


def _cond(c, w, b, col_tile):
    n_layers, d, n = w.shape
    batch = c.shape[0]
    return pl.pallas_call(
        _cond_kernel,
        grid=(n_layers, n // col_tile),
        in_specs=[
            pl.BlockSpec((batch, d), lambda l, j: (0, 0)),
            pl.BlockSpec((1, d, col_tile), lambda l, j: (l, 0, j)),
            pl.BlockSpec((1, 1, col_tile), lambda l, j: (l, 0, j)),
        ],
        out_specs=pl.BlockSpec((1, batch, col_tile), lambda l, j: (l, 0, j)),
        out_shape=jax.ShapeDtypeStruct((n_layers, batch, n), F32),
        compiler_params=_params("parallel", "parallel"),
        name="cond",
    )(c, w, b.reshape(n_layers, 1, n))


def _ffn_kernel(x_ref, shift_ref, scale_ref, gate_ref, gpre_ref, gpost_ref,
                wg_ref, wu_ref, wd_ref, o_ref, act_ref, *, res_weight):
    d_ff = wg_ref.shape[1]
    group = x_ref.shape[1] // FFN_ROW_SPLIT
    for r in range(FFN_ROW_SPLIT):
        rows = slice(r * group, (r + 1) * group)
        x = x_ref[0, rows, :]
        h = _modulated_norm(x, gpre_ref[...], scale_ref[0], shift_ref[0]).astype(BF16)
        for j in range(d_ff // FF_CHUNK):
            cols = slice(j * FF_CHUNK, (j + 1) * FF_CHUNK)
            g = jnp.dot(h, wg_ref[:, cols], preferred_element_type=F32)
            u = jnp.dot(h, wu_ref[:, cols], preferred_element_type=F32)
            act_ref[rows, cols] = (_silu(g) * u).astype(BF16)
        y = jnp.dot(act_ref[rows, :], wd_ref[...], preferred_element_type=F32)
        o_ref[0, rows, :] = x + (res_weight * gate_ref[0]) * _rms(y, gpost_ref[...])


def _ffn(x, shift, scale, gate, g_pre, g_post, w_g, w_u, w_d, res_weight):
    batch, seq, d = x.shape
    d_ff = w_g.shape[1]
    tile = FFN_TILE

    def resident(shape):
        return pl.BlockSpec(shape, lambda b, i: (0, 0), pipeline_mode=pl.Buffered(1))

    return pl.pallas_call(
        functools.partial(_ffn_kernel, res_weight=res_weight),
        grid=(batch, seq // tile),
        in_specs=[
            _tok_spec(tile, d),
            _batch_vec_spec(d), _batch_vec_spec(d), _batch_vec_spec(d),
            _const_spec((1, d)), _const_spec((1, d)),
            resident((d, d_ff)), resident((d, d_ff)), resident((d_ff, d)),
        ],
        out_specs=_tok_spec(tile, d),
        out_shape=jax.ShapeDtypeStruct(x.shape, x.dtype),
        scratch_shapes=[pltpu.VMEM((tile, d_ff), BF16)],
        compiler_params=_params("parallel", "parallel"),
        name="ffn",
    )(x, shift, scale, gate, g_pre, g_post, w_g, w_u, w_d)


def _gla_kernel(x_ref, shift_ref, scale_ref, gate_ref, gpre_ref, gpost_ref,
                wq_ref, wk_ref, wv_ref, wlow_ref, wr_ref, wup_ref, bgate_ref,
                gout_ref, wout_ref, o_ref,
                state_ref, qd_ref, kd_ref, ku_ref, v_ref, dec_ref, mix_ref):
    tile = x_ref.shape[1]
    n_chunks = tile // GLA_CHUNK
    kd_all = wq_ref.shape[1]
    dk = kd_all // GLA_HEADS
    dv = wv_ref.shape[1] // GLA_HEADS

    @pl.when(pl.program_id(1) == 0)
    def _():
        state_ref[...] = jnp.zeros_like(state_ref)

    x = x_ref[0]
    h = _modulated_norm(x, gpre_ref[...], scale_ref[0], shift_ref[0]).astype(BF16)
    q = jnp.dot(h, wq_ref[...], preferred_element_type=F32) * dk ** -0.5
    k = jnp.dot(h, wk_ref[...], preferred_element_type=F32)
    v_ref[...] = jnp.dot(h, wv_ref[...], preferred_element_type=F32).astype(BF16)
    low = jnp.dot(h, wlow_ref[...], preferred_element_type=F32).astype(BF16)
    z = jnp.dot(low, wup_ref[...], preferred_element_type=F32) + bgate_ref[...]
    log_a = (jnp.minimum(z, 0.0) - jnp.log(1.0 + jnp.exp(-jnp.abs(z)))) / GLA_TAU

    row = lax.broadcasted_iota(jnp.int32, (tile, tile), 0)
    col = lax.broadcasted_iota(jnp.int32, (tile, tile), 1)
    tri = jnp.where((col <= row) & (col // GLA_CHUNK == row // GLA_CHUNK),
                    1.0, 0.0).astype(BF16)
    la_hi = log_a.astype(BF16)
    la_lo = (log_a - la_hi.astype(F32)).astype(BF16)
    b = (jnp.dot(tri, la_hi, preferred_element_type=F32) +
         jnp.dot(tri, la_lo, preferred_element_type=F32))
    b3 = b.reshape(n_chunks, GLA_CHUNK, kd_all)
    b_last = b3[:, GLA_CHUNK - 1:, :]
    qd_ref[...] = (q * jnp.exp(b)).astype(BF16)
    kd_ref[...] = (k * jnp.exp(-b)).astype(BF16)
    ku_ref[...] = (k.reshape(n_chunks, GLA_CHUNK, kd_all) *
                   jnp.exp(b_last - b3)).reshape(tile, kd_all).astype(BF16)
    dec_ref[...] = jnp.exp(b_last)

    ri = lax.broadcasted_iota(jnp.int32, (GLA_CHUNK, GLA_CHUNK), 0)
    ci = lax.broadcasted_iota(jnp.int32, (GLA_CHUNK, GLA_CHUNK), 1)
    causal = ci <= ri

    def chunk_step(n, carry):
        rows = pl.ds(pl.multiple_of(n * GLA_CHUNK, GLA_CHUNK), GLA_CHUNK)
        dec = dec_ref[n]
        for hd in range(GLA_HEADS):
            kcols = slice(hd * dk, (hd + 1) * dk)
            vcols = slice(hd * dv, (hd + 1) * dv)
            qd = qd_ref[rows, kcols]
            kd = kd_ref[rows, kcols]
            ku = ku_ref[rows, kcols]
            vc = v_ref[rows, vcols]
            att = lax.dot_general(qd, kd, (((1,), (1,)), ((), ())),
                                  preferred_element_type=F32)
            att = jnp.where(causal, att, 0.0).astype(BF16)
            state_t = state_ref[hd]
            o = jnp.dot(att, vc, preferred_element_type=F32)
            o = o + lax.dot_general(qd, state_t.astype(BF16),
                                    (((1,), (1,)), ((), ())),
                                    preferred_element_type=F32)
            mix_ref[rows, vcols] = o
            upd_t = lax.dot_general(vc, ku, (((0,), (0,)), ((), ())),
                                    preferred_element_type=F32)
            state_ref[hd] = dec[:, kcols] * state_t + upd_t
        return carry

    lax.fori_loop(0, n_chunks, chunk_step, 0)

    r = jnp.dot(h, wr_ref[...], preferred_element_type=F32)
    for hd in range(GLA_HEADS):
        vcols = slice(hd * dv, (hd + 1) * dv)
        mix_ref[:, vcols] = _rms(mix_ref[:, vcols], gout_ref[...])
    y = (mix_ref[...] * _silu(r)).astype(BF16)
    y = jnp.dot(y, wout_ref[...], preferred_element_type=F32)
    o_ref[0] = x + gate_ref[0] * _rms(y, gpost_ref[...])


def _gla(x, shift, scale, gate, g_pre, g_post, w_q, w_k, w_v, w_low, w_r, w_up,
         b_gate, g_out, w_out):
    batch, seq, d = x.shape
    tile = TOKEN_TILE
    kd_all = w_q.shape[1]
    vd_all = w_v.shape[1]
    dk = kd_all // GLA_HEADS
    dv = vd_all // GLA_HEADS
    weights = (w_q, w_k, w_v, w_low, w_r, w_up, b_gate, g_out, w_out)
    return pl.pallas_call(
        _gla_kernel,
        grid=(batch, seq // tile),
        in_specs=[
            _tok_spec(tile, d),
            _batch_vec_spec(d), _batch_vec_spec(d), _batch_vec_spec(d),
            _const_spec((1, d)), _const_spec((1, d)),
        ] + [_const_spec(w.shape) for w in weights],
        out_specs=_tok_spec(tile, d),
        out_shape=jax.ShapeDtypeStruct(x.shape, x.dtype),
        scratch_shapes=[
            pltpu.VMEM((GLA_HEADS, dv, dk), F32),
            pltpu.VMEM((tile, kd_all), BF16),
            pltpu.VMEM((tile, kd_all), BF16),
            pltpu.VMEM((tile, kd_all), BF16),
            pltpu.VMEM((tile, vd_all), BF16),
            pltpu.VMEM((tile // GLA_CHUNK, 1, kd_all), F32),
            pltpu.VMEM((tile, vd_all), F32),
        ],
        compiler_params=_params("parallel", "arbitrary"),
        name="gla",
    )(x, shift, scale, gate, g_pre, g_post, *weights)


def _swap_halves(x):
    half = x.shape[-1] // 2
    return jnp.concatenate([x[:, half:], x[:, :half]], axis=-1)


def _rope(x, cos2, sin2):
    return x * cos2 + _swap_halves(x) * sin2


def _kv_kernel(x_ref, shift_ref, scale_ref, pos_ref, freq_ref, gin_ref,
               wc_ref, wpe_ref, gkv_ref, wbk_ref, wbvt_ref,
               k_ref, vt_ref, cos_ref, sin_ref):
    x = x_ref[0]
    h = _modulated_norm(x, gin_ref[...], scale_ref[0], shift_ref[0]).astype(BF16)
    c_kv = jnp.dot(h, wc_ref[...], preferred_element_type=F32)
    k_pe = jnp.dot(h, wpe_ref[...], preferred_element_type=F32)
    c_kv = _rms(c_kv, gkv_ref[...]).astype(BF16)
    k_nope = jnp.dot(c_kv, wbk_ref[...], preferred_element_type=F32)
    vt = lax.dot_general(wbvt_ref[...], c_kv, (((1,), (1,)), ((), ())),
                         preferred_element_type=F32).astype(BF16)
    for hd in range(MLA_HEADS):
        for r in range(vt_ref.shape[2]):
            vt_ref[0, hd, r] = vt[hd * V_HEAD:(hd + 1) * V_HEAD,
                                  r * ATTN_KEY_TILE:(r + 1) * ATTN_KEY_TILE]

    ang = pos_ref[0].astype(F32) * freq_ref[...]
    cos = jnp.cos(ang)
    sin = jnp.sin(ang)
    cos2 = jnp.concatenate([cos, cos], axis=-1)
    sin2 = jnp.concatenate([-sin, sin], axis=-1)
    cos_ref[0] = cos2.T
    sin_ref[0] = sin2.T
    k_rope = _rope(k_pe, cos2, sin2).astype(BF16)
    for hd in range(MLA_HEADS):
        k_ref[0, hd, :, :QK_NOPE] = k_nope[:, hd * QK_NOPE:(hd + 1) * QK_NOPE].astype(BF16)
        k_ref[0, hd, :, QK_NOPE:] = k_rope


def _shared_kv(x, shift, scale, pos, inv_freq, g_in, w_c, w_pe, g_kv, w_bk, w_bvt):
    batch, seq, d = x.shape
    tile = TOKEN_TILE
    qk = QK_NOPE + QK_ROPE
    weights = (g_in, w_c, w_pe, g_kv, w_bk, w_bvt)

    def transposed_spec(rows):
        return pl.BlockSpec((1, rows, tile), lambda b, i: (b, 0, i))

    return pl.pallas_call(
        _kv_kernel,
        grid=(batch, seq // tile),
        in_specs=[
            _tok_spec(tile, d), _batch_vec_spec(d), _batch_vec_spec(d),
            _tok_spec(tile, 1), _const_spec(inv_freq.shape),
        ] + [_const_spec(w.shape) for w in weights],
        out_specs=[
            pl.BlockSpec((1, MLA_HEADS, tile, qk), lambda b, i: (b, 0, i, 0)),
            pl.BlockSpec((1, MLA_HEADS, tile // ATTN_KEY_TILE, V_HEAD, ATTN_KEY_TILE),
                         lambda b, i: (b, 0, i, 0, 0)),
            transposed_spec(QK_ROPE), transposed_spec(QK_ROPE),
        ],
        out_shape=[
            jax.ShapeDtypeStruct((batch, MLA_HEADS, seq, qk), BF16),
            jax.ShapeDtypeStruct((batch, MLA_HEADS, seq // ATTN_KEY_TILE, V_HEAD, ATTN_KEY_TILE),
                                 BF16),
            jax.ShapeDtypeStruct((batch, QK_ROPE, seq), F32),
            jax.ShapeDtypeStruct((batch, QK_ROPE, seq), F32),
        ],
        compiler_params=_params("parallel", "parallel"),
        name="shared_kv",
    )(x, shift, scale, pos, inv_freq, *weights)


def _swap_halves_rows(x):
    half = x.shape[0] // 2
    return jnp.concatenate([x[half:], x[:half]], axis=0)


def _q_kernel(x_ref, shift_ref, scale_ref, cos_ref, sin_ref, gpre_ref,
              wdq_ref, gq_ref, wnt_ref, wrt_ref, q_ref):
    x = x_ref[0]
    h = _modulated_norm(x, gpre_ref[...], scale_ref[0], shift_ref[0]).astype(BF16)
    c_q = jnp.dot(h, wdq_ref[...], preferred_element_type=F32)
    c_q = _rms(c_q, gq_ref[...]).astype(BF16)
    q_scale = (QK_NOPE + QK_ROPE) ** -0.5 * LOG2E
    nt = (((1,), (1,)), ((), ()))
    q_nope = lax.dot_general(wnt_ref[...], c_q, nt, preferred_element_type=F32) * q_scale
    q_rope = lax.dot_general(wrt_ref[...], c_q, nt, preferred_element_type=F32) * q_scale
    cos2 = cos_ref[0]
    sin2 = sin_ref[0]
    for hd in range(MLA_HEADS):
        q_ref[0, hd, :QK_NOPE, :] = q_nope[hd * QK_NOPE:(hd + 1) * QK_NOPE].astype(BF16)
        qr = q_rope[hd * QK_ROPE:(hd + 1) * QK_ROPE]
        q_ref[0, hd, QK_NOPE:, :] = (qr * cos2 + _swap_halves_rows(qr) * sin2).astype(BF16)


def _mla_q(x, shift, scale, cos2t, sin2t, g_pre, w_dq, g_q, w_nt, w_rt):
    batch, seq, d = x.shape
    tile = TOKEN_TILE
    qk = QK_NOPE + QK_ROPE
    weights = (g_pre, w_dq, g_q, w_nt, w_rt)
    rope_spec = pl.BlockSpec((1, QK_ROPE, tile), lambda b, i: (b, 0, i))
    return pl.pallas_call(
        _q_kernel,
        grid=(batch, seq // tile),
        in_specs=[
            _tok_spec(tile, d), _batch_vec_spec(d), _batch_vec_spec(d),
            rope_spec, rope_spec,
        ] + [_const_spec(w.shape) for w in weights],
        out_specs=pl.BlockSpec((1, MLA_HEADS, qk, tile), lambda b, i: (b, 0, 0, i)),
        out_shape=jax.ShapeDtypeStruct((batch, MLA_HEADS, qk, seq), BF16),
        compiler_params=_params("parallel", "parallel"),
        name="mla_q",
    )(x, shift, scale, cos2t, sin2t, *weights)


def _attn_kernel(qt_ref, k_ref, vt_ref, ot_ref, m_ref, l_ref, acc_ref):
    group = qt_ref.shape[1]
    tq = qt_ref.shape[3]
    dv = vt_ref.shape[3]
    tk = ATTN_KEY_TILE
    qi = pl.program_id(2)
    neg = jnp.finfo(F32).min

    m_ref[...] = jnp.full_like(m_ref, -jnp.inf)
    l_ref[...] = jnp.zeros_like(l_ref)
    acc_ref[...] = jnp.zeros_like(acc_ref)

    def scores(hd, key_tile, q_start):
        rows = pl.ds(pl.multiple_of(key_tile * tk, tk), tk)
        return jnp.dot(k_ref[0, hd, rows, :], qt_ref[0, hd, :, q_start:],
                       preferred_element_type=F32)

    def update(hd, key_tile, q_start, s, masked):
        qcols = slice(q_start, tq)
        if masked:
            ki = lax.broadcasted_iota(jnp.int32, s.shape, 0)
            qj = lax.broadcasted_iota(jnp.int32, s.shape, 1)
            s = jnp.where(ki <= qj, s, neg)
        m_prev = m_ref[hd, :, qcols]
        m_new = jnp.maximum(m_prev, jnp.max(s, axis=0, keepdims=True))
        alpha = jnp.exp2(m_prev - m_new)
        p = jnp.exp2(s - m_new)
        l_ref[hd, :, qcols] = alpha * l_ref[hd, :, qcols] + jnp.sum(p, axis=0, keepdims=True)
        m_ref[hd, :, qcols] = m_new
        acc_ref[hd, :, qcols] = alpha * acc_ref[hd, :, qcols] + jnp.dot(
            vt_ref[0, hd, key_tile], p.astype(BF16), preferred_element_type=F32)

    tiles_per_span = tq // tk

    def span(first_tile, masked):
        steps = [(hd, first_tile + r, r * tk if masked else 0)
                 for r in range(tiles_per_span) for hd in range(group)]
        pending = [scores(*st) for st in steps[:ATTN_LOOKAHEAD]]
        for idx, st in enumerate(steps):
            if idx + ATTN_LOOKAHEAD < len(steps):
                pending.append(scores(*steps[idx + ATTN_LOOKAHEAD]))
            update(*st, pending.pop(0), masked)

    def full_span(j, carry):
        span(j * tiles_per_span, False)
        return carry

    lax.fori_loop(0, qi, full_span, 0)
    span(qi * tiles_per_span, True)
    for hd in range(group):
        ot_ref[0, hd * dv:(hd + 1) * dv, :] = (acc_ref[hd] / l_ref[hd]).astype(ot_ref.dtype)


def _attention(qt, k, vt):
    batch, heads, qk, seq = qt.shape
    _, _, n_key_tiles, dv, key_tile = vt.shape
    tile = ATTN_TILE
    group = ATTN_HEAD_GROUP
    return pl.pallas_call(
        _attn_kernel,
        grid=(batch, heads // group, seq // tile),
        in_specs=[
            pl.BlockSpec((1, group, qk, tile), lambda b, g, i: (b, g, 0, i)),
            pl.BlockSpec((1, group, seq, qk), lambda b, g, i: (b, g, 0, 0)),
            pl.BlockSpec((1, group, n_key_tiles, dv, key_tile),
                         lambda b, g, i: (b, g, 0, 0, 0)),
        ],
        out_specs=pl.BlockSpec((1, group * dv, tile), lambda b, g, i: (b, g, i)),
        out_shape=jax.ShapeDtypeStruct((batch, heads * dv, seq), BF16),
        scratch_shapes=[
            pltpu.VMEM((group, 1, tile), F32),
            pltpu.VMEM((group, 1, tile), F32),
            pltpu.VMEM((group, dv, tile), F32),
        ],
        compiler_params=_params("parallel", "parallel", "parallel"),
        name="mla_attention",
    )(qt, k, vt)


def _proj_out_kernel(x_ref, at_ref, gate_ref, gpost_ref, w_ref, o_ref):
    y = lax.dot_general(at_ref[0], w_ref[...], (((0,), (0,)), ((), ())),
                        preferred_element_type=F32)
    o_ref[0] = x_ref[0] + gate_ref[0] * _rms(y, gpost_ref[...])


def _proj_out(x, at, gate, g_post, w_out):
    batch, seq, d = x.shape
    tile = TOKEN_TILE
    return pl.pallas_call(
        _proj_out_kernel,
        grid=(batch, seq // tile),
        in_specs=[
            _tok_spec(tile, d),
            pl.BlockSpec((1, at.shape[1], tile), lambda b, i: (b, 0, i)),
            _batch_vec_spec(d),
            _const_spec((1, d)), _const_spec(w_out.shape),
        ],
        out_specs=_tok_spec(tile, d),
        out_shape=jax.ShapeDtypeStruct(x.shape, x.dtype),
        compiler_params=_params("parallel", "parallel"),
        name="mla_out",
    )(x, at, gate, g_post, w_out)


def kernel(x, c, positions, cond_w, cond_b, norm_g, ffn_w_gu, ffn_w_down, gla_w_in, gla_w_gate_up, gla_b_gate, gla_g_out, gla_w_out, kv_g_in, kv_cond_w, kv_cond_b, mla_w_kv_a, mla_g_kv, mla_w_kv_b, mla_w_dq, mla_g_q, mla_w_uq, mla_w_out):
    batch, seq, d = x.shape
    depth = cond_w.shape[0]
    n_a = gla_w_in.shape[0]
    d_ff = ffn_w_down.shape[2]
    kd = GLA_HEADS * (d // 2 // GLA_HEADS)
    vd = d

    n_mod = 3 * N_SUBLAYERS
    mods = _cond(c, cond_w, cond_b, cond_w.shape[2] // 4)
    mods = mods.reshape(depth, batch, n_mod, 1, d)
    kv_mods = _cond(c, kv_cond_w[None], kv_cond_b[None], kv_cond_w.shape[1] // 2)
    kv_mods = kv_mods.reshape(batch, 2, 1, d)

    def mod(layer, idx):
        return mods[layer, :, idx]

    def gain(layer, sub, which):
        return norm_g[layer, sub, which].reshape(1, d)

    def ffn(x, layer, half, sub):
        w_gu = ffn_w_gu[layer, half].astype(BF16)
        return _ffn(x, mod(layer, 3 * sub), mod(layer, 3 * sub + 1), mod(layer, 3 * sub + 2),
                    gain(layer, sub, 0), gain(layer, sub, 1),
                    w_gu[:, :d_ff], w_gu[:, d_ff:],
                    ffn_w_down[layer, half].astype(BF16), 0.5)

    inv_freq = (ROPE_THETA ** (-jnp.arange(0, QK_ROPE, 2, dtype=F32) / QK_ROPE)
                ).reshape(1, QK_ROPE // 2)
    pos = positions.reshape(batch, seq, 1)

    shared = None
    for layer in range(depth):
        x = ffn(x, layer, 0, 0)
        if layer < n_a:
            i = layer
            w_in = gla_w_in[i].astype(BF16)
            w_low = jnp.zeros((d, LANES), BF16).at[:, :GLA_GATE_RANK].set(
                w_in[:, 2 * kd + vd:2 * kd + vd + GLA_GATE_RANK])
            w_up = jnp.zeros((LANES, kd), BF16).at[:GLA_GATE_RANK].set(
                gla_w_gate_up[i].astype(BF16))
            x = _gla(x, mod(layer, 3), mod(layer, 4), mod(layer, 5),
                     gain(layer, 1, 0), gain(layer, 1, 1),
                     w_in[:, :kd], w_in[:, kd:2 * kd], w_in[:, 2 * kd:2 * kd + vd],
                     w_low, w_in[:, 2 * kd + vd + GLA_GATE_RANK:], w_up,
                     gla_b_gate[i].reshape(1, kd), gla_g_out[i].reshape(1, -1),
                     gla_w_out[i].astype(BF16))
        else:
            i = layer - n_a
            k_all, vt_all, cos2t, sin2t = shared
            w_uq = mla_w_uq[i].astype(BF16).reshape(-1, MLA_HEADS, QK_NOPE + QK_ROPE)
            qt = _mla_q(x, mod(layer, 3), mod(layer, 4), cos2t, sin2t,
                        gain(layer, 1, 0), mla_w_dq[i].astype(BF16),
                        mla_g_q[i].reshape(1, -1),
                        w_uq[:, :, :QK_NOPE].reshape(-1, MLA_HEADS * QK_NOPE).T,
                        w_uq[:, :, QK_NOPE:].reshape(-1, MLA_HEADS * QK_ROPE).T)
            at = _attention(qt, k_all, vt_all)
            x = _proj_out(x, at, mod(layer, 5), gain(layer, 1, 1),
                          mla_w_out[i].astype(BF16))
        x = ffn(x, layer, 1, 2)
        if layer == n_a - 1:
            w_kv_a = mla_w_kv_a.astype(BF16)
            w_kv_b = mla_w_kv_b.astype(BF16).reshape(KV_LORA, MLA_HEADS, QK_NOPE + V_HEAD)
            shared = _shared_kv(
                x, kv_mods[:, 0], kv_mods[:, 1], pos, inv_freq,
                kv_g_in.reshape(1, d), w_kv_a[:, :KV_LORA], w_kv_a[:, KV_LORA:],
                mla_g_kv.reshape(1, KV_LORA),
                w_kv_b[:, :, :QK_NOPE].reshape(KV_LORA, MLA_HEADS * QK_NOPE),
                w_kv_b[:, :, QK_NOPE:].reshape(KV_LORA, MLA_HEADS * V_HEAD).T)
    return x
```

```python
import functools

import jax
import jax.numpy as jnp
from jax import lax
from jax.experimental import pallas as pl
from jax.experimental.pallas import tpu as pltpu

F32 = jnp.float32
BF16 = jnp.bfloat16

EPS = 1e-6
N_SUBLAYERS = 3
GLA_HEADS = 4
GLA_GATE_RANK = 16
GLA_TAU = 16.0
GLA_CHUNK = 64
GLA_BLOCK = 256
MLA_HEADS = 8
QK_NOPE = 128
QK_ROPE = 64
V_HEAD = 128
KV_LORA = 256
ROPE_THETA = 10000.0

LANES = 128
VMEM_LIMIT_BYTES = 56 * 1024 * 1024

TOKEN_TILE = 512
FFN_TILE = 1024
FFN_ROW_SPLIT = 2
FF_CHUNK = 256
ATTN_TILE = 512
ATTN_KEY_TILE = 256
ATTN_HEAD_GROUP = 4
ATTN_LOOKAHEAD = 2
LOG2E = 1.4426950408889634


def _params(*sem):
    return pltpu.CompilerParams(dimension_semantics=sem,
                                vmem_limit_bytes=VMEM_LIMIT_BYTES)


def _rms(x, g):
    return x * lax.rsqrt(jnp.mean(x * x, axis=-1, keepdims=True) + EPS) * g


def _silu(x):
    return x * jax.nn.sigmoid(x)


def _modulated_norm(x, g, scale, shift):
    return _rms(x, g) * (1.0 + scale) + shift


def _const_spec(shape):
    nd = len(shape)
    return pl.BlockSpec(shape, lambda *_: (0,) * nd)


def _tok_spec(tile, width):
    return pl.BlockSpec((1, tile, width), lambda b, i: (b, i, 0))


def _batch_vec_spec(width):
    return pl.BlockSpec((1, 1, width), lambda b, i: (b, 0, 0))


def _cond_kernel(c_ref, w_ref, b_ref, o_ref):
    c_act = _silu(c_ref[...])
    o_ref[0] = jnp.dot(c_act, w_ref[0], preferred_element_type=F32) + b_ref[0]


def _cond(c, w, b, col_tile):
    n_layers, d, n = w.shape
    batch = c.shape[0]
    return pl.pallas_call(
        _cond_kernel,
        grid=(n_layers, n // col_tile),
        in_specs=[
            pl.BlockSpec((batch, d), lambda l, j: (0, 0)),
            pl.BlockSpec((1, d, col_tile), lambda l, j: (l, 0, j)),
            pl.BlockSpec((1, 1, col_tile), lambda l, j: (l, 0, j)),
        ],
        out_specs=pl.BlockSpec((1, batch, col_tile), lambda l, j: (l, 0, j)),
        out_shape=jax.ShapeDtypeStruct((n_layers, batch, n), F32),
        compiler_params=_params("parallel", "parallel"),
        name="cond",
    )(c, w, b.reshape(n_layers, 1, n))


def _ffn_kernel(x_ref, shift_ref, scale_ref, gate_ref, gpre_ref, gpost_ref,
                wg_ref, wu_ref, wd_ref, o_ref, act_ref, *, res_weight):
    d_ff = wg_ref.shape[1]
    group = x_ref.shape[1] // FFN_ROW_SPLIT
    for r in range(FFN_ROW_SPLIT):
        rows = slice(r * group, (r + 1) * group)
        x = x_ref[0, rows, :]
        h = _modulated_norm(x, gpre_ref[...], scale_ref[0], shift_ref[0]).astype(BF16)
        for j in range(d_ff // FF_CHUNK):
            cols = slice(j * FF_CHUNK, (j + 1) * FF_CHUNK)
            g = jnp.dot(h, wg_ref[:, cols], preferred_element_type=F32)
            u = jnp.dot(h, wu_ref[:, cols], preferred_element_type=F32)
            act_ref[rows, cols] = (_silu(g) * u).astype(BF16)
        y = jnp.dot(act_ref[rows, :], wd_ref[...], preferred_element_type=F32)
        o_ref[0, rows, :] = x + (res_weight * gate_ref[0]) * _rms(y, gpost_ref[...])


def _ffn(x, shift, scale, gate, g_pre, g_post, w_gu, w_d, res_weight):
    batch, seq, d = x.shape
    d_ff = w_d.shape[0]
    tile = FFN_TILE

    def resident(shape, col_block=0):
        return pl.BlockSpec(shape, lambda b, i: (0, col_block),
                            pipeline_mode=pl.Buffered(1))

    return pl.pallas_call(
        functools.partial(_ffn_kernel, res_weight=res_weight),
        grid=(batch, seq // tile),
        in_specs=[
            _tok_spec(tile, d),
            _batch_vec_spec(d), _batch_vec_spec(d), _batch_vec_spec(d),
            _const_spec((1, d)), _const_spec((1, d)),
            resident((d, d_ff), 0), resident((d, d_ff), 1), resident((d_ff, d)),
        ],
        out_specs=_tok_spec(tile, d),
        out_shape=jax.ShapeDtypeStruct(x.shape, x.dtype),
        scratch_shapes=[pltpu.VMEM((tile, d_ff), BF16)],
        compiler_params=_params("parallel", "parallel"),
        name="ffn",
    )(x, shift, scale, gate, g_pre, g_post, w_gu, w_gu, w_d)


def _gla_kernel(x_ref, shift_ref, scale_ref, gate_ref, gpre_ref, gpost_ref,
                wq_ref, wk_ref, wv_ref, wlow_ref, wr_ref, wup_ref, bgate_ref,
                gout_ref, wout_ref, o_ref,
                state_ref, qd_ref, kd_ref, kut_ref, v_ref, dect_ref, upd_ref, mix_ref):
    tile = x_ref.shape[1]
    n_chunks = tile // GLA_CHUNK
    n_blocks = tile // GLA_BLOCK
    chunks_per_block = GLA_BLOCK // GLA_CHUNK
    kd_all = wq_ref.shape[1]
    dk = kd_all // GLA_HEADS
    dv = wv_ref.shape[1] // GLA_HEADS

    @pl.when(pl.program_id(1) == 0)
    def _():
        state_ref[...] = jnp.zeros_like(state_ref)

    x = x_ref[0]
    h = _modulated_norm(x, gpre_ref[...], scale_ref[0], shift_ref[0]).astype(BF16)
    low = jnp.dot(h, wlow_ref[...], preferred_element_type=F32).astype(BF16)
    z = jnp.dot(low, wup_ref[...], preferred_element_type=F32) + bgate_ref[...]
    log_a = (jnp.minimum(z, 0.0) - jnp.log(1.0 + jnp.exp(-jnp.abs(z)))) / GLA_TAU
    q = jnp.dot(h, wq_ref[...], preferred_element_type=F32) * dk ** -0.5
    k = jnp.dot(h, wk_ref[...], preferred_element_type=F32)

    row = lax.broadcasted_iota(jnp.int32, (GLA_BLOCK, GLA_BLOCK), 0)
    col = lax.broadcasted_iota(jnp.int32, (GLA_BLOCK, GLA_BLOCK), 1)
    chunk_causal = (col <= row) & (col // GLA_CHUNK == row // GLA_CHUNK)

    tri = jnp.where(chunk_causal, 1.0, 0.0).astype(BF16)
    la_hi = log_a.astype(BF16)
    la_lo = (log_a - la_hi.astype(F32)).astype(BF16)
    b = jnp.concatenate([
        jnp.dot(tri, la_hi[i * GLA_BLOCK:(i + 1) * GLA_BLOCK], preferred_element_type=F32) +
        jnp.dot(tri, la_lo[i * GLA_BLOCK:(i + 1) * GLA_BLOCK], preferred_element_type=F32)
        for i in range(n_blocks)], axis=0)
    v_ref[...] = jnp.dot(h, wv_ref[...], preferred_element_type=F32).astype(BF16)
    r = jnp.dot(h, wr_ref[...], preferred_element_type=F32)
    b3 = b.reshape(n_chunks, GLA_CHUNK, kd_all)
    b_last = b3[:, GLA_CHUNK - 1:, :]
    qd_ref[...] = (q * jnp.exp(b)).astype(BF16)
    kd_ref[...] = (k * jnp.exp(-b)).astype(BF16)
    k_upd = (k.reshape(n_chunks, GLA_CHUNK, kd_all) * jnp.exp(b_last - b3)).reshape(tile, kd_all)
    kut_ref[...] = k_upd.T.astype(BF16)
    dec = jnp.exp(b_last).reshape(n_chunks, kd_all)
    dec = jnp.concatenate([dec, jnp.zeros((LANES - n_chunks, kd_all), F32)], axis=0)
    dect_ref[...] = dec.T

    lane_chunk = lax.broadcasted_iota(jnp.int32, (dk, GLA_BLOCK), 1) // GLA_CHUNK
    nt = (((1,), (1,)), ((), ()))

    steps = [(hd, blk) for blk in range(n_blocks) for hd in range(GLA_HEADS)]

    def attention_block(hd, blk):
        rows = slice(blk * GLA_BLOCK, (blk + 1) * GLA_BLOCK)
        kcols = slice(hd * dk, (hd + 1) * dk)
        return lax.dot_general(qd_ref[rows, kcols], kd_ref[rows, kcols], nt,
                               preferred_element_type=F32)

    def value_block(hd, blk, att):
        rows = slice(blk * GLA_BLOCK, (blk + 1) * GLA_BLOCK)
        vcols = slice(hd * dv, (hd + 1) * dv)
        kut = kut_ref[hd * dk:(hd + 1) * dk, rows]
        lhs = [jnp.where(chunk_causal, att, 0.0).astype(BF16)]
        for n in range(chunks_per_block):
            lhs.append(jnp.where(lane_chunk == n, kut, jnp.zeros_like(kut)))
        res = jnp.dot(jnp.concatenate(lhs, axis=0), v_ref[rows, vcols],
                      preferred_element_type=F32)
        mix_ref[rows, vcols] = res[:GLA_BLOCK]
        for n in range(chunks_per_block):
            upd_ref[hd, blk * chunks_per_block + n] = res[GLA_BLOCK + n * dk:
                                                         GLA_BLOCK + (n + 1) * dk]

    pending = attention_block(*steps[0])
    for idx, st in enumerate(steps):
        nxt = attention_block(*steps[idx + 1]) if idx + 1 < len(steps) else None
        value_block(*st, pending)
        pending = nxt

    for n in range(n_chunks):
        rows = slice(n * GLA_CHUNK, (n + 1) * GLA_CHUNK)
        for hd in range(GLA_HEADS):
            kcols = slice(hd * dk, (hd + 1) * dk)
            vcols = slice(hd * dv, (hd + 1) * dv)
            state = state_ref[hd]
            mix_ref[rows, vcols] += jnp.dot(qd_ref[rows, kcols], state.astype(BF16),
                                            preferred_element_type=F32)
            state_ref[hd] = dect_ref[kcols, n:n + 1] * state + upd_ref[hd, n]

    for hd in range(GLA_HEADS):
        vcols = slice(hd * dv, (hd + 1) * dv)
        mix_ref[:, vcols] = _rms(mix_ref[:, vcols], gout_ref[...])
    y = (mix_ref[...] * _silu(r)).astype(BF16)
    y = jnp.dot(y, wout_ref[...], preferred_element_type=F32)
    o_ref[0] = x + gate_ref[0] * _rms(y, gpost_ref[...])


def _gla(x, shift, scale, gate, g_pre, g_post, w_q, w_k, w_v, w_low, w_r, w_up,
         b_gate, g_out, w_out):
    batch, seq, d = x.shape
    tile = TOKEN_TILE
    kd_all = w_q.shape[1]
    vd_all = w_v.shape[1]
    dk = kd_all // GLA_HEADS
    dv = vd_all // GLA_HEADS
    weights = (w_q, w_k, w_v, w_low, w_r, w_up, b_gate, g_out, w_out)
    return pl.pallas_call(
        _gla_kernel,
        grid=(batch, seq // tile),
        in_specs=[
            _tok_spec(tile, d),
            _batch_vec_spec(d), _batch_vec_spec(d), _batch_vec_spec(d),
            _const_spec((1, d)), _const_spec((1, d)),
        ] + [_const_spec(w.shape) for w in weights],
        out_specs=_tok_spec(tile, d),
        out_shape=jax.ShapeDtypeStruct(x.shape, x.dtype),
        scratch_shapes=[
            pltpu.VMEM((GLA_HEADS, dk, dv), F32),
            pltpu.VMEM((tile, kd_all), BF16),
            pltpu.VMEM((tile, kd_all), BF16),
            pltpu.VMEM((kd_all, tile), BF16),
            pltpu.VMEM((tile, vd_all), BF16),
            pltpu.VMEM((kd_all, LANES), F32),
            pltpu.VMEM((GLA_HEADS, tile // GLA_CHUNK, dk, dv), F32),
            pltpu.VMEM((tile, vd_all), F32),
        ],
        compiler_params=_params("parallel", "arbitrary"),
        name="gla",
    )(x, shift, scale, gate, g_pre, g_post, *weights)


def _swap_halves(x):
    half = x.shape[-1] // 2
    return jnp.concatenate([x[:, half:], x[:, :half]], axis=-1)


def _rope(x, cos2, sin2):
    return x * cos2 + _swap_halves(x) * sin2


def _kv_kernel(x_ref, shift_ref, scale_ref, pos_ref, freq_ref, gin_ref,
               wc_ref, wpe_ref, gkv_ref, wbk_ref, wbvt_ref,
               k_ref, vt_ref, cos_ref, sin_ref):
    x = x_ref[0]
    h = _modulated_norm(x, gin_ref[...], scale_ref[0], shift_ref[0]).astype(BF16)
    c_kv = jnp.dot(h, wc_ref[...], preferred_element_type=F32)
    k_pe = jnp.dot(h, wpe_ref[...], preferred_element_type=F32)
    c_kv = _rms(c_kv, gkv_ref[...]).astype(BF16)
    k_nope = jnp.dot(c_kv, wbk_ref[...], preferred_element_type=F32)
    vt = lax.dot_general(wbvt_ref[...], c_kv, (((1,), (1,)), ((), ())),
                         preferred_element_type=F32).astype(BF16)
    for hd in range(MLA_HEADS):
        for r in range(vt_ref.shape[2]):
            vt_ref[0, hd, r] = vt[hd * V_HEAD:(hd + 1) * V_HEAD,
                                  r * ATTN_KEY_TILE:(r + 1) * ATTN_KEY_TILE]

    ang = freq_ref[...] * pos_ref[0].astype(F32)
    cos = jnp.cos(ang)
    sin = jnp.sin(ang)
    cos2t = jnp.concatenate([cos, cos], axis=0)
    sin2t = jnp.concatenate([-sin, sin], axis=0)
    cos_ref[0] = cos2t
    sin_ref[0] = sin2t
    k_rope = _rope(k_pe, cos2t.T, sin2t.T).astype(BF16)
    for hd in range(MLA_HEADS):
        k_ref[0, hd, :, :QK_NOPE] = k_nope[:, hd * QK_NOPE:(hd + 1) * QK_NOPE].astype(BF16)
        k_ref[0, hd, :, QK_NOPE:] = k_rope


def _shared_kv(x, shift, scale, pos, inv_freq, g_in, w_c, w_pe, g_kv, w_bk, w_bvt):
    batch, seq, d = x.shape
    tile = TOKEN_TILE
    qk = QK_NOPE + QK_ROPE
    weights = (g_in, w_c, w_pe, g_kv, w_bk, w_bvt)

    def transposed_spec(rows):
        return pl.BlockSpec((1, rows, tile), lambda b, i: (b, 0, i))

    return pl.pallas_call(
        _kv_kernel,
        grid=(batch, seq // tile),
        in_specs=[
            _tok_spec(tile, d), _batch_vec_spec(d), _batch_vec_spec(d),
            transposed_spec(1), _const_spec(inv_freq.shape),
        ] + [_const_spec(w.shape) for w in weights],
        out_specs=[
            pl.BlockSpec((1, MLA_HEADS, tile, qk), lambda b, i: (b, 0, i, 0)),
            pl.BlockSpec((1, MLA_HEADS, tile // ATTN_KEY_TILE, V_HEAD, ATTN_KEY_TILE),
                         lambda b, i: (b, 0, i, 0, 0)),
            transposed_spec(QK_ROPE), transposed_spec(QK_ROPE),
        ],
        out_shape=[
            jax.ShapeDtypeStruct((batch, MLA_HEADS, seq, qk), BF16),
            jax.ShapeDtypeStruct((batch, MLA_HEADS, seq // ATTN_KEY_TILE, V_HEAD, ATTN_KEY_TILE),
                                 BF16),
            jax.ShapeDtypeStruct((batch, QK_ROPE, seq), F32),
            jax.ShapeDtypeStruct((batch, QK_ROPE, seq), F32),
        ],
        compiler_params=_params("parallel", "parallel"),
        name="shared_kv",
    )(x, shift, scale, pos, inv_freq, *weights)


def _swap_halves_rows(x):
    half = x.shape[0] // 2
    return jnp.concatenate([x[half:], x[:half]], axis=0)


def _q_kernel(x_ref, shift_ref, scale_ref, cos_ref, sin_ref, gpre_ref,
              wdq_ref, gq_ref, wnt_ref, wrt_ref, q_ref):
    x = x_ref[0]
    h = _modulated_norm(x, gpre_ref[...], scale_ref[0], shift_ref[0]).astype(BF16)
    c_q = jnp.dot(h, wdq_ref[...], preferred_element_type=F32)
    c_q = _rms(c_q, gq_ref[...]).astype(BF16)
    q_scale = (QK_NOPE + QK_ROPE) ** -0.5 * LOG2E
    nt = (((1,), (1,)), ((), ()))
    q_nope = lax.dot_general(wnt_ref[...], c_q, nt, preferred_element_type=F32) * q_scale
    q_rope = lax.dot_general(wrt_ref[...], c_q, nt, preferred_element_type=F32) * q_scale
    cos2 = cos_ref[0]
    sin2 = sin_ref[0]
    for hd in range(MLA_HEADS):
        q_ref[0, hd, :QK_NOPE, :] = q_nope[hd * QK_NOPE:(hd + 1) * QK_NOPE].astype(BF16)
        qr = q_rope[hd * QK_ROPE:(hd + 1) * QK_ROPE]
        q_ref[0, hd, QK_NOPE:, :] = (qr * cos2 + _swap_halves_rows(qr) * sin2).astype(BF16)


def _mla_q(x, shift, scale, cos2t, sin2t, g_pre, w_dq, g_q, w_nt, w_rt):
    batch, seq, d = x.shape
    tile = TOKEN_TILE
    qk = QK_NOPE + QK_ROPE
    weights = (g_pre, w_dq, g_q, w_nt, w_rt)
    rope_spec = pl.BlockSpec((1, QK_ROPE, tile), lambda b, i: (b, 0, i))
    return pl.pallas_call(
        _q_kernel,
        grid=(batch, seq // tile),
        in_specs=[
            _tok_spec(tile, d), _batch_vec_spec(d), _batch_vec_spec(d),
            rope_spec, rope_spec,
        ] + [_const_spec(w.shape) for w in weights],
        out_specs=pl.BlockSpec((1, MLA_HEADS, qk, tile), lambda b, i: (b, 0, 0, i)),
        out_shape=jax.ShapeDtypeStruct((batch, MLA_HEADS, qk, seq), BF16),
        compiler_params=_params("parallel", "parallel"),
        name="mla_q",
    )(x, shift, scale, cos2t, sin2t, *weights)


def _attn_kernel(qt_ref, k_ref, vt_ref, ot_ref, m_ref, l_ref, acc_ref, ahead_ref):
    group = qt_ref.shape[1]
    tq = qt_ref.shape[3]
    dv = vt_ref.shape[3]
    tk = ATTN_KEY_TILE
    qi = pl.program_id(2)
    neg = jnp.finfo(F32).min

    m_ref[...] = jnp.full_like(m_ref, -jnp.inf)
    l_ref[...] = jnp.zeros_like(l_ref)
    acc_ref[...] = jnp.zeros_like(acc_ref)

    def scores(hd, key_tile, q_start):
        rows = pl.ds(pl.multiple_of(key_tile * tk, tk), tk)
        return jnp.dot(k_ref[0, hd, rows, :], qt_ref[0, hd, :, q_start:],
                       preferred_element_type=F32)

    def update(hd, key_tile, q_start, s, masked):
        qcols = slice(q_start, tq)
        if masked:
            ki = lax.broadcasted_iota(jnp.int32, s.shape, 0)
            qj = lax.broadcasted_iota(jnp.int32, s.shape, 1)
            s = jnp.where(ki <= qj, s, neg)
        m_prev = m_ref[hd, :, qcols]
        m_new = jnp.maximum(m_prev, jnp.max(s, axis=0, keepdims=True))
        alpha = jnp.exp2(m_prev - m_new)
        p = jnp.exp2(s - m_new)
        l_ref[hd, :, qcols] = alpha * l_ref[hd, :, qcols] + jnp.sum(p, axis=0, keepdims=True)
        m_ref[hd, :, qcols] = m_new
        acc_ref[hd, :, qcols] = alpha * acc_ref[hd, :, qcols] + jnp.dot(
            vt_ref[0, hd, key_tile], p.astype(BF16), preferred_element_type=F32)

    tiles_per_span = tq // tk

    def span(first_tile, masked):
        steps = [(hd, first_tile + r, r * tk if masked else 0)
                 for r in range(tiles_per_span) for hd in range(group)]
        pending = [ahead_ref[i] for i in range(ATTN_LOOKAHEAD)]
        for idx, st in enumerate(steps):
            ahead = idx + ATTN_LOOKAHEAD
            if ahead < len(steps):
                pending.append(scores(*steps[ahead]))
            elif not masked:
                ahead_ref[ahead - len(steps)] = scores(
                    ahead - len(steps), first_tile + tiles_per_span, 0)
            update(*st, pending.pop(0), masked)

    for i in range(ATTN_LOOKAHEAD):
        ahead_ref[i] = scores(i, 0, 0)

    def full_span(j, carry):
        span(j * tiles_per_span, False)
        return carry

    lax.fori_loop(0, qi, full_span, 0)
    span(qi * tiles_per_span, True)
    for hd in range(group):
        ot_ref[0, hd * dv:(hd + 1) * dv, :] = (acc_ref[hd] / l_ref[hd]).astype(ot_ref.dtype)


def _attention(qt, k, vt):
    batch, heads, qk, seq = qt.shape
    _, _, n_key_tiles, dv, key_tile = vt.shape
    tile = ATTN_TILE
    group = ATTN_HEAD_GROUP
    return pl.pallas_call(
        _attn_kernel,
        grid=(batch, heads // group, seq // tile),
        in_specs=[
            pl.BlockSpec((1, group, qk, tile), lambda b, g, i: (b, g, 0, i)),
            pl.BlockSpec((1, group, seq, qk), lambda b, g, i: (b, g, 0, 0)),
            pl.BlockSpec((1, group, n_key_tiles, dv, key_tile),
                         lambda b, g, i: (b, g, 0, 0, 0)),
        ],
        out_specs=pl.BlockSpec((1, group * dv, tile), lambda b, g, i: (b, g, i)),
        out_shape=jax.ShapeDtypeStruct((batch, heads * dv, seq), BF16),
        scratch_shapes=[
            pltpu.VMEM((group, 1, tile), F32),
            pltpu.VMEM((group, 1, tile), F32),
            pltpu.VMEM((group, dv, tile), F32),
            pltpu.VMEM((ATTN_LOOKAHEAD, key_tile, tile), F32),
        ],
        compiler_params=_params("parallel", "parallel", "parallel"),
        name="mla_attention",
    )(qt, k, vt)


def _proj_out_kernel(x_ref, at_ref, gate_ref, gpost_ref, w_ref, o_ref):
    y = lax.dot_general(at_ref[0], w_ref[...], (((0,), (0,)), ((), ())),
                        preferred_element_type=F32)
    o_ref[0] = x_ref[0] + gate_ref[0] * _rms(y, gpost_ref[...])


def _proj_out(x, at, gate, g_post, w_out):
    batch, seq, d = x.shape
    tile = TOKEN_TILE
    return pl.pallas_call(
        _proj_out_kernel,
        grid=(batch, seq // tile),
        in_specs=[
            _tok_spec(tile, d),
            pl.BlockSpec((1, at.shape[1], tile), lambda b, i: (b, 0, i)),
            _batch_vec_spec(d),
            _const_spec((1, d)), _const_spec(w_out.shape),
        ],
        out_specs=_tok_spec(tile, d),
        out_shape=jax.ShapeDtypeStruct(x.shape, x.dtype),
        compiler_params=_params("parallel", "parallel"),
        name="mla_out",
    )(x, at, gate, g_post, w_out)


def kernel(x, c, positions, cond_w, cond_b, norm_g, ffn_w_gu, ffn_w_down, gla_w_in, gla_w_gate_up, gla_b_gate, gla_g_out, gla_w_out, kv_g_in, kv_cond_w, kv_cond_b, mla_w_kv_a, mla_g_kv, mla_w_kv_b, mla_w_dq, mla_g_q, mla_w_uq, mla_w_out):
    batch, seq, d = x.shape
    depth = cond_w.shape[0]
    n_a = gla_w_in.shape[0]
    d_ff = ffn_w_down.shape[2]
    kd = GLA_HEADS * (d // 2 // GLA_HEADS)
    vd = d

    n_mod = 3 * N_SUBLAYERS
    mods = _cond(c, cond_w, cond_b, cond_w.shape[2] // 4)
    mods = mods.reshape(depth, batch, n_mod, 1, d)
    kv_mods = _cond(c, kv_cond_w[None], kv_cond_b[None], kv_cond_w.shape[1] // 2)
    kv_mods = kv_mods.reshape(batch, 2, 1, d)

    def mod(layer, idx):
        return mods[layer, :, idx]

    def gain(layer, sub, which):
        return norm_g[layer, sub, which].reshape(1, d)

    def ffn(x, layer, half, sub):
        w_gu = ffn_w_gu[layer, half].astype(BF16)
        return _ffn(x, mod(layer, 3 * sub), mod(layer, 3 * sub + 1), mod(layer, 3 * sub + 2),
                    gain(layer, sub, 0), gain(layer, sub, 1),
                    w_gu,
                    ffn_w_down[layer, half].astype(BF16), 0.5)

    inv_freq = (ROPE_THETA ** (-jnp.arange(0, QK_ROPE, 2, dtype=F32) / QK_ROPE)
                ).reshape(QK_ROPE // 2, 1)
    pos = positions.reshape(batch, 1, seq)

    shared = None
    for layer in range(depth):
        x = ffn(x, layer, 0, 0)
        if layer < n_a:
            i = layer
            w_in = gla_w_in[i].astype(BF16)
            w_low = jnp.zeros((d, LANES), BF16).at[:, :GLA_GATE_RANK].set(
                w_in[:, 2 * kd + vd:2 * kd + vd + GLA_GATE_RANK])
            w_up = jnp.zeros((LANES, kd), BF16).at[:GLA_GATE_RANK].set(
                gla_w_gate_up[i].astype(BF16))
            x = _gla(x, mod(layer, 3), mod(layer, 4), mod(layer, 5),
                     gain(layer, 1, 0), gain(layer, 1, 1),
                     w_in[:, :kd], w_in[:, kd:2 * kd], w_in[:, 2 * kd:2 * kd + vd],
                     w_low, w_in[:, 2 * kd + vd + GLA_GATE_RANK:], w_up,
                     gla_b_gate[i].reshape(1, kd), gla_g_out[i].reshape(1, -1),
                     gla_w_out[i].astype(BF16))
        else:
            i = layer - n_a
            k_all, vt_all, cos2t, sin2t = shared
            w_uq = mla_w_uq[i].astype(BF16).reshape(-1, MLA_HEADS, QK_NOPE + QK_ROPE)
            qt = _mla_q(x, mod(layer, 3), mod(layer, 4), cos2t, sin2t,
                        gain(layer, 1, 0), mla_w_dq[i].astype(BF16),
                        mla_g_q[i].reshape(1, -1),
                        w_uq[:, :, :QK_NOPE].reshape(-1, MLA_HEADS * QK_NOPE).T,
                        w_uq[:, :, QK_NOPE:].reshape(-1, MLA_HEADS * QK_ROPE).T)
            at = _attention(qt, k_all, vt_all)
            x = _proj_out(x, at, mod(layer, 5), gain(layer, 1, 1),
                          mla_w_out[i].astype(BF16))
        x = ffn(x, layer, 1, 2)
        if layer == n_a - 1:
            w_kv_a = mla_w_kv_a.astype(BF16)
            w_kv_b = mla_w_kv_b.astype(BF16).reshape(KV_LORA, MLA_HEADS, QK_NOPE + V_HEAD)
            shared = _shared_kv(
                x, kv_mods[:, 0], kv_mods[:, 1], pos, inv_freq,
                kv_g_in.reshape(1, d), w_kv_a[:, :KV_LORA], w_kv_a[:, KV_LORA:],
                mla_g_kv.reshape(1, KV_LORA),
                w_kv_b[:, :, :QK_NOPE].reshape(KV_LORA, MLA_HEADS * QK_NOPE),
                w_kv_b[:, :, QK_NOPE:].reshape(KV_LORA, MLA_HEADS * V_HEAD).T)
    return x
```

```python
import functools

import jax
import jax.numpy as jnp
from jax import lax
from jax.experimental import pallas as pl
from jax.experimental.pallas import tpu as pltpu

F32 = jnp.float32
BF16 = jnp.bfloat16

EPS = 1e-6
N_SUBLAYERS = 3
GLA_HEADS = 4
GLA_GATE_RANK = 16
GLA_TAU = 16.0
GLA_CHUNK = 64
GLA_BLOCK = 256
MLA_HEADS = 8
QK_NOPE = 128
QK_ROPE = 64
V_HEAD = 128
V_ONES_ROWS = 16
KV_LORA = 256
ROPE_THETA = 10000.0

LANES = 128
VMEM_LIMIT_BYTES = 56 * 1024 * 1024

TOKEN_TILE = 512
FFN_TILE = 1024
FFN_ROW_SPLIT = 2
FF_CHUNK = 256
ATTN_TILE = 512
ATTN_KEY_TILE = 256
ATTN_HEAD_GROUP = 4
ATTN_LOOKAHEAD = 2
LOG2E = 1.4426950408889634


def _params(*sem):
    return pltpu.CompilerParams(dimension_semantics=sem,
                                vmem_limit_bytes=VMEM_LIMIT_BYTES)


def _rms(x, g):
    return x * lax.rsqrt(jnp.mean(x * x, axis=-1, keepdims=True) + EPS) * g


def _silu(x):
    return x * jax.nn.sigmoid(x)


def _modulated_norm(x, g, scale, shift):
    return _rms(x, g) * (1.0 + scale) + shift


def _const_spec(shape):
    nd = len(shape)
    return pl.BlockSpec(shape, lambda *_: (0,) * nd)


def _tok_spec(tile, width):
    return pl.BlockSpec((1, tile, width), lambda b, i: (b, i, 0))


def _batch_vec_spec(width):
    return pl.BlockSpec((1, 1, width), lambda b, i: (b, 0, 0))


def _cond_kernel(c_ref, w_ref, b_ref, o_ref):
    c_act = _silu(c_ref[...])
    o_ref[0] = jnp.dot(c_act, w_ref[0], preferred_element_type=F32) + b_ref[0]


def _cond(c, w, b, col_tile):
    n_layers, d, n = w.shape
    batch = c.shape[0]
    return pl.pallas_call(
        _cond_kernel,
        grid=(n_layers, n // col_tile),
        in_specs=[
            pl.BlockSpec((batch, d), lambda l, j: (0, 0)),
            pl.BlockSpec((1, d, col_tile), lambda l, j: (l, 0, j)),
            pl.BlockSpec((1, 1, col_tile), lambda l, j: (l, 0, j)),
        ],
        out_specs=pl.BlockSpec((1, batch, col_tile), lambda l, j: (l, 0, j)),
        out_shape=jax.ShapeDtypeStruct((n_layers, batch, n), F32),
        compiler_params=_params("parallel", "parallel"),
        name="cond",
    )(c, w, b.reshape(n_layers, 1, n))


def _ffn_kernel(x_ref, shift_ref, scale_ref, gate_ref, gpre_ref, gpost_ref,
                wg_ref, wu_ref, wd_ref, *rest, res_weight, mixer_tail):
    if mixer_tail:
        at_ref, mgate_ref, mgpost_ref, wout_ref, o_ref, act_ref = rest
    else:
        o_ref, act_ref = rest
    d_ff = wg_ref.shape[1]
    group = x_ref.shape[1] // FFN_ROW_SPLIT
    for r in range(FFN_ROW_SPLIT):
        rows = slice(r * group, (r + 1) * group)
        x = x_ref[0, rows, :]
        if mixer_tail:
            y = lax.dot_general(at_ref[0, :, rows], wout_ref[...], (((0,), (0,)), ((), ())),
                                preferred_element_type=F32)
            x = x + mgate_ref[0] * _rms(y, mgpost_ref[...])
        o_ref[0, rows, :] = x
        h = _modulated_norm(x, gpre_ref[...], scale_ref[0], shift_ref[0]).astype(BF16)
        for j in range(d_ff // FF_CHUNK):
            cols = slice(j * FF_CHUNK, (j + 1) * FF_CHUNK)
            g = jnp.dot(h, wg_ref[:, cols], preferred_element_type=F32)
            u = jnp.dot(h, wu_ref[:, cols], preferred_element_type=F32)
            act_ref[rows, cols] = (_silu(g) * u).astype(BF16)
        y = jnp.dot(act_ref[rows, :], wd_ref[...], preferred_element_type=F32)
        o_ref[0, rows, :] = (o_ref[0, rows, :] +
                             (res_weight * gate_ref[0]) * _rms(y, gpost_ref[...]))


def _ffn(x, shift, scale, gate, g_pre, g_post, w_gu, w_d, layer, half, res_weight,
         mixer_tail=None):
    batch, seq, d = x.shape
    d_ff = w_d.shape[2]
    tile = FFN_TILE

    def resident(shape, col_block=0):
        return pl.BlockSpec((None, None) + shape, lambda b, i: (layer, half, 0, col_block),
                            pipeline_mode=pl.Buffered(1))

    operands = [x, shift, scale, gate, g_pre, g_post, w_gu, w_gu, w_d]
    in_specs = [
        _tok_spec(tile, d),
        _batch_vec_spec(d), _batch_vec_spec(d), _batch_vec_spec(d),
        _const_spec((1, d)), _const_spec((1, d)),
        resident((d, d_ff), 0), resident((d, d_ff), 1), resident((d_ff, d)),
    ]
    if mixer_tail is not None:
        at, m_gate, m_gpost, w_out = mixer_tail
        operands += [at, m_gate, m_gpost, w_out]
        in_specs += [
            pl.BlockSpec((1, at.shape[1], tile), lambda b, i: (b, 0, i)),
            _batch_vec_spec(d), _const_spec((1, d)),
            pl.BlockSpec(w_out.shape, lambda b, i: (0, 0), pipeline_mode=pl.Buffered(1)),
        ]
    return pl.pallas_call(
        functools.partial(_ffn_kernel, res_weight=res_weight,
                          mixer_tail=mixer_tail is not None),
        grid=(batch, seq // tile),
        in_specs=in_specs,
        out_specs=_tok_spec(tile, d),
        out_shape=jax.ShapeDtypeStruct(x.shape, x.dtype),
        scratch_shapes=[pltpu.VMEM((tile, d_ff), BF16)],
        compiler_params=_params("parallel", "parallel"),
        name="ffn_mixer_tail" if mixer_tail is not None else "ffn",
    )(*operands)


def _gla_kernel(x_ref, shift_ref, scale_ref, gate_ref, gpre_ref, gpost_ref,
                wq_ref, wk_ref, wv_ref, wlow_ref, wr_ref, wup_ref, bgate_ref,
                gout_ref, wout_ref, o_ref,
                state_ref, qd_ref, kd_ref, kut_ref, v_ref, dect_ref, upd_ref, mix_ref):
    tile = x_ref.shape[1]
    n_chunks = tile // GLA_CHUNK
    n_blocks = tile // GLA_BLOCK
    chunks_per_block = GLA_BLOCK // GLA_CHUNK
    kd_all = wq_ref.shape[1]
    dk = kd_all // GLA_HEADS
    dv = wv_ref.shape[1] // GLA_HEADS

    @pl.when(pl.program_id(1) == 0)
    def _():
        state_ref[...] = jnp.zeros_like(state_ref)

    x = x_ref[0]
    h = _modulated_norm(x, gpre_ref[...], scale_ref[0], shift_ref[0]).astype(BF16)
    low = jnp.dot(h, wlow_ref[...], preferred_element_type=F32).astype(BF16)
    z = jnp.dot(low, wup_ref[...], preferred_element_type=F32) + bgate_ref[...]
    log_a = (jnp.minimum(z, 0.0) - jnp.log(1.0 + jnp.exp(-jnp.abs(z)))) / GLA_TAU
    q = jnp.dot(h, wq_ref[...], preferred_element_type=F32) * dk ** -0.5
    k = jnp.dot(h, wk_ref[...], preferred_element_type=F32)

    row = lax.broadcasted_iota(jnp.int32, (GLA_BLOCK, GLA_BLOCK), 0)
    col = lax.broadcasted_iota(jnp.int32, (GLA_BLOCK, GLA_BLOCK), 1)
    chunk_causal = (col <= row) & (col // GLA_CHUNK == row // GLA_CHUNK)

    tri = jnp.where(chunk_causal, 1.0, 0.0).astype(BF16)
    la_hi = log_a.astype(BF16)
    la_lo = (log_a - la_hi.astype(F32)).astype(BF16)
    b = jnp.concatenate([
        jnp.dot(tri, la_hi[i * GLA_BLOCK:(i + 1) * GLA_BLOCK], preferred_element_type=F32) +
        jnp.dot(tri, la_lo[i * GLA_BLOCK:(i + 1) * GLA_BLOCK], preferred_element_type=F32)
        for i in range(n_blocks)], axis=0)
    v_ref[...] = jnp.dot(h, wv_ref[...], preferred_element_type=F32).astype(BF16)
    r = jnp.dot(h, wr_ref[...], preferred_element_type=F32)
    b3 = b.reshape(n_chunks, GLA_CHUNK, kd_all)
    b_last = b3[:, GLA_CHUNK - 1:, :]
    qd_ref[...] = (q * jnp.exp(b)).astype(BF16)
    kd_ref[...] = (k * jnp.exp(-b)).astype(BF16)
    k_upd = (k.reshape(n_chunks, GLA_CHUNK, kd_all) * jnp.exp(b_last - b3)).reshape(tile, kd_all)
    kut_ref[...] = k_upd.T.astype(BF16)
    dec = jnp.exp(b_last).reshape(n_chunks, kd_all)
    dec = jnp.concatenate([dec, jnp.zeros((LANES - n_chunks, kd_all), F32)], axis=0)
    dect_ref[...] = dec.T

    lane_chunk = lax.broadcasted_iota(jnp.int32, (dk, GLA_BLOCK), 1) // GLA_CHUNK
    nt = (((1,), (1,)), ((), ()))

    steps = [(hd, blk) for blk in range(n_blocks) for hd in range(GLA_HEADS)]

    def attention_block(hd, blk):
        rows = slice(blk * GLA_BLOCK, (blk + 1) * GLA_BLOCK)
        kcols = slice(hd * dk, (hd + 1) * dk)
        return lax.dot_general(qd_ref[rows, kcols], kd_ref[rows, kcols], nt,
                               preferred_element_type=F32)

    def value_block(hd, blk, att):
        rows = slice(blk * GLA_BLOCK, (blk + 1) * GLA_BLOCK)
        vcols = slice(hd * dv, (hd + 1) * dv)
        kut = kut_ref[hd * dk:(hd + 1) * dk, rows]
        lhs = [jnp.where(chunk_causal, att, 0.0).astype(BF16)]
        for n in range(chunks_per_block):
            lhs.append(jnp.where(lane_chunk == n, kut, jnp.zeros_like(kut)))
        res = jnp.dot(jnp.concatenate(lhs, axis=0), v_ref[rows, vcols],
                      preferred_element_type=F32)
        mix_ref[rows, vcols] = res[:GLA_BLOCK]
        for n in range(chunks_per_block):
            upd_ref[hd, blk * chunks_per_block + n] = res[GLA_BLOCK + n * dk:
                                                         GLA_BLOCK + (n + 1) * dk]

    pending = attention_block(*steps[0])
    for idx, st in enumerate(steps):
        nxt = attention_block(*steps[idx + 1]) if idx + 1 < len(steps) else None
        value_block(*st, pending)
        pending = nxt

    for n in range(n_chunks):
        rows = slice(n * GLA_CHUNK, (n + 1) * GLA_CHUNK)
        for hd in range(GLA_HEADS):
            kcols = slice(hd * dk, (hd + 1) * dk)
            vcols = slice(hd * dv, (hd + 1) * dv)
            state = state_ref[hd]
            mix_ref[rows, vcols] += jnp.dot(qd_ref[rows, kcols], state.astype(BF16),
                                            preferred_element_type=F32)
            state_ref[hd] = dect_ref[kcols, n:n + 1] * state + upd_ref[hd, n]

    for hd in range(GLA_HEADS):
        vcols = slice(hd * dv, (hd + 1) * dv)
        mix_ref[:, vcols] = _rms(mix_ref[:, vcols], gout_ref[...])
    y = (mix_ref[...] * _silu(r)).astype(BF16)
    y = jnp.dot(y, wout_ref[...], preferred_element_type=F32)
    o_ref[0] = x + gate_ref[0] * _rms(y, gpost_ref[...])


def _gla(x, shift, scale, gate, g_pre, g_post, w_q, w_k, w_v, w_low, w_r, w_up,
         b_gate, g_out, w_out):
    batch, seq, d = x.shape
    tile = TOKEN_TILE
    kd_all = w_q.shape[1]
    vd_all = w_v.shape[1]
    dk = kd_all // GLA_HEADS
    dv = vd_all // GLA_HEADS
    weights = (w_q, w_k, w_v, w_low, w_r, w_up, b_gate, g_out, w_out)
    return pl.pallas_call(
        _gla_kernel,
        grid=(batch, seq // tile),
        in_specs=[
            _tok_spec(tile, d),
            _batch_vec_spec(d), _batch_vec_spec(d), _batch_vec_spec(d),
            _const_spec((1, d)), _const_spec((1, d)),
        ] + [_const_spec(w.shape) for w in weights],
        out_specs=_tok_spec(tile, d),
        out_shape=jax.ShapeDtypeStruct(x.shape, x.dtype),
        scratch_shapes=[
            pltpu.VMEM((GLA_HEADS, dk, dv), F32),
            pltpu.VMEM((tile, kd_all), BF16),
            pltpu.VMEM((tile, kd_all), BF16),
            pltpu.VMEM((kd_all, tile), BF16),
            pltpu.VMEM((tile, vd_all), BF16),
            pltpu.VMEM((kd_all, LANES), F32),
            pltpu.VMEM((GLA_HEADS, tile // GLA_CHUNK, dk, dv), F32),
            pltpu.VMEM((tile, vd_all), F32),
        ],
        compiler_params=_params("parallel", "arbitrary"),
        name="gla",
    )(x, shift, scale, gate, g_pre, g_post, *weights)


def _swap_halves(x):
    half = x.shape[-1] // 2
    return jnp.concatenate([x[:, half:], x[:, :half]], axis=-1)


def _rope(x, cos2, sin2):
    return x * cos2 + _swap_halves(x) * sin2


def _kv_kernel(x_ref, shift_ref, scale_ref, pos_ref, freq_ref, gin_ref,
               wc_ref, wpe_ref, gkv_ref, wbk_ref, wbvt_ref,
               k_ref, vt_ref, cos_ref, sin_ref):
    x = x_ref[0]
    h = _modulated_norm(x, gin_ref[...], scale_ref[0], shift_ref[0]).astype(BF16)
    c_kv = jnp.dot(h, wc_ref[...], preferred_element_type=F32)
    k_pe = jnp.dot(h, wpe_ref[...], preferred_element_type=F32)
    c_kv = _rms(c_kv, gkv_ref[...]).astype(BF16)
    k_nope = jnp.dot(c_kv, wbk_ref[...], preferred_element_type=F32)
    vt = lax.dot_general(wbvt_ref[...], c_kv, (((1,), (1,)), ((), ())),
                         preferred_element_type=F32).astype(BF16)
    ones = jnp.ones((V_ONES_ROWS, ATTN_KEY_TILE), BF16)
    for hd in range(MLA_HEADS):
        for r in range(vt_ref.shape[2]):
            vt_ref[0, hd, r, :V_HEAD, :] = vt[hd * V_HEAD:(hd + 1) * V_HEAD,
                                              r * ATTN_KEY_TILE:(r + 1) * ATTN_KEY_TILE]
            vt_ref[0, hd, r, V_HEAD:, :] = ones

    ang = freq_ref[...] * pos_ref[0].astype(F32)
    cos = jnp.cos(ang)
    sin = jnp.sin(ang)
    cos2t = jnp.concatenate([cos, cos], axis=0)
    sin2t = jnp.concatenate([-sin, sin], axis=0)
    cos_ref[0] = cos2t
    sin_ref[0] = sin2t
    k_rope = _rope(k_pe, cos2t.T, sin2t.T).astype(BF16)
    for hd in range(MLA_HEADS):
        k_ref[0, hd, :, :QK_NOPE] = k_nope[:, hd * QK_NOPE:(hd + 1) * QK_NOPE].astype(BF16)
        k_ref[0, hd, :, QK_NOPE:] = k_rope


def _shared_kv(x, shift, scale, pos, inv_freq, g_in, w_c, w_pe, g_kv, w_bk, w_bvt):
    batch, seq, d = x.shape
    tile = TOKEN_TILE
    qk = QK_NOPE + QK_ROPE
    weights = (g_in, w_c, w_pe, g_kv, w_bk, w_bvt)

    def transposed_spec(rows):
        return pl.BlockSpec((1, rows, tile), lambda b, i: (b, 0, i))

    return pl.pallas_call(
        _kv_kernel,
        grid=(batch, seq // tile),
        in_specs=[
            _tok_spec(tile, d), _batch_vec_spec(d), _batch_vec_spec(d),
            transposed_spec(1), _const_spec(inv_freq.shape),
        ] + [_const_spec(w.shape) for w in weights],
        out_specs=[
            pl.BlockSpec((1, MLA_HEADS, tile, qk), lambda b, i: (b, 0, i, 0)),
            pl.BlockSpec((1, MLA_HEADS, tile // ATTN_KEY_TILE, V_HEAD + V_ONES_ROWS,
                          ATTN_KEY_TILE), lambda b, i: (b, 0, i, 0, 0)),
            transposed_spec(QK_ROPE), transposed_spec(QK_ROPE),
        ],
        out_shape=[
            jax.ShapeDtypeStruct((batch, MLA_HEADS, seq, qk), BF16),
            jax.ShapeDtypeStruct((batch, MLA_HEADS, seq // ATTN_KEY_TILE, V_HEAD + V_ONES_ROWS,
                                  ATTN_KEY_TILE), BF16),
            jax.ShapeDtypeStruct((batch, QK_ROPE, seq), F32),
            jax.ShapeDtypeStruct((batch, QK_ROPE, seq), F32),
        ],
        compiler_params=_params("parallel", "parallel"),
        name="shared_kv",
    )(x, shift, scale, pos, inv_freq, *weights)


def _swap_halves_rows(x):
    half = x.shape[0] // 2
    return jnp.concatenate([x[half:], x[:half]], axis=0)


def _q_kernel(x_ref, shift_ref, scale_ref, cos_ref, sin_ref, gpre_ref,
              wdq_ref, gq_ref, wnt_ref, wrt_ref, q_ref):
    x = x_ref[0]
    h = _modulated_norm(x, gpre_ref[...], scale_ref[0], shift_ref[0]).astype(BF16)
    c_q = jnp.dot(h, wdq_ref[...], preferred_element_type=F32)
    c_q = _rms(c_q, gq_ref[...]).astype(BF16)
    q_scale = (QK_NOPE + QK_ROPE) ** -0.5 * LOG2E
    nt = (((1,), (1,)), ((), ()))
    q_nope = lax.dot_general(wnt_ref[...], c_q, nt, preferred_element_type=F32) * q_scale
    q_rope = lax.dot_general(wrt_ref[...], c_q, nt, preferred_element_type=F32) * q_scale
    cos2 = cos_ref[0]
    sin2 = sin_ref[0]
    for hd in range(MLA_HEADS):
        q_ref[0, hd, :QK_NOPE, :] = q_nope[hd * QK_NOPE:(hd + 1) * QK_NOPE].astype(BF16)
        qr = q_rope[hd * QK_ROPE:(hd + 1) * QK_ROPE]
        q_ref[0, hd, QK_NOPE:, :] = (qr * cos2 + _swap_halves_rows(qr) * sin2).astype(BF16)


def _mla_q(x, shift, scale, cos2t, sin2t, g_pre, w_dq, g_q, w_nt, w_rt):
    batch, seq, d = x.shape
    tile = TOKEN_TILE
    qk = QK_NOPE + QK_ROPE
    weights = (g_pre, w_dq, g_q, w_nt, w_rt)
    rope_spec = pl.BlockSpec((1, QK_ROPE, tile), lambda b, i: (b, 0, i))
    return pl.pallas_call(
        _q_kernel,
        grid=(batch, seq // tile),
        in_specs=[
            _tok_spec(tile, d), _batch_vec_spec(d), _batch_vec_spec(d),
            rope_spec, rope_spec,
        ] + [_const_spec(w.shape) for w in weights],
        out_specs=pl.BlockSpec((1, MLA_HEADS, qk, tile), lambda b, i: (b, 0, 0, i)),
        out_shape=jax.ShapeDtypeStruct((batch, MLA_HEADS, qk, seq), BF16),
        compiler_params=_params("parallel", "parallel"),
        name="mla_q",
    )(x, shift, scale, cos2t, sin2t, *weights)


def _attn_kernel(qt_ref, k_ref, vt_ref, ot_ref, m_ref, acc_ref, ahead_ref):
    group = qt_ref.shape[1]
    tq = qt_ref.shape[3]
    dv = V_HEAD
    tk = ATTN_KEY_TILE
    qi = pl.program_id(2)
    neg = jnp.finfo(F32).min

    m_ref[...] = jnp.full_like(m_ref, -jnp.inf)
    acc_ref[...] = jnp.zeros_like(acc_ref)

    def scores(hd, key_tile, q_start):
        rows = pl.ds(pl.multiple_of(key_tile * tk, tk), tk)
        return jnp.dot(k_ref[0, hd, rows, :], qt_ref[0, hd, :, q_start:],
                       preferred_element_type=F32)

    def update(hd, key_tile, q_start, s, masked):
        qcols = slice(q_start, tq)
        if masked:
            ki = lax.broadcasted_iota(jnp.int32, s.shape, 0)
            qj = lax.broadcasted_iota(jnp.int32, s.shape, 1)
            s = jnp.where(ki <= qj, s, neg)
        m_prev = m_ref[hd, :, qcols]
        m_new = jnp.maximum(m_prev, jnp.max(s, axis=0, keepdims=True))
        alpha = jnp.exp2(m_prev - m_new)
        p = jnp.exp2(s - m_new)
        m_ref[hd, :, qcols] = m_new
        acc_ref[hd, :, qcols] = alpha * acc_ref[hd, :, qcols] + jnp.dot(
            vt_ref[0, hd, key_tile], p.astype(BF16), preferred_element_type=F32)

    tiles_per_span = tq // tk

    def span(first_tile, masked):
        steps = [(hd, first_tile + r, r * tk if masked else 0)
                 for r in range(tiles_per_span) for hd in range(group)]
        pending = [ahead_ref[i] for i in range(ATTN_LOOKAHEAD)]
        for idx, st in enumerate(steps):
            ahead = idx + ATTN_LOOKAHEAD
            if ahead < len(steps):
                pending.append(scores(*steps[ahead]))
            elif not masked:
                ahead_ref[ahead - len(steps)] = scores(
                    ahead - len(steps), first_tile + tiles_per_span, 0)
            update(*st, pending.pop(0), masked)

    for i in range(ATTN_LOOKAHEAD):
        ahead_ref[i] = scores(i, 0, 0)

    def full_span(j, carry):
        span(j * tiles_per_span, False)
        return carry

    lax.fori_loop(0, qi, full_span, 0)
    span(qi * tiles_per_span, True)
    for hd in range(group):
        ot_ref[0, hd * dv:(hd + 1) * dv, :] = (
            acc_ref[hd, :dv, :] / acc_ref[hd, dv:dv + 1, :]).astype(ot_ref.dtype)


def _attention(qt, k, vt):
    batch, heads, qk, seq = qt.shape
    _, _, n_key_tiles, dv_aug, key_tile = vt.shape
    dv = V_HEAD
    tile = ATTN_TILE
    group = ATTN_HEAD_GROUP
    return pl.pallas_call(
        _attn_kernel,
        grid=(batch, heads // group, seq // tile),
        in_specs=[
            pl.BlockSpec((1, group, qk, tile), lambda b, g, i: (b, g, 0, i)),
            pl.BlockSpec((1, group, seq, qk), lambda b, g, i: (b, g, 0, 0)),
            pl.BlockSpec((1, group, n_key_tiles, dv_aug, key_tile),
                         lambda b, g, i: (b, g, 0, 0, 0)),
        ],
        out_specs=pl.BlockSpec((1, group * dv, tile), lambda b, g, i: (b, g, i)),
        out_shape=jax.ShapeDtypeStruct((batch, heads * dv, seq), BF16),
        scratch_shapes=[
            pltpu.VMEM((group, 1, tile), F32),
            pltpu.VMEM((group, dv_aug, tile), F32),
            pltpu.VMEM((ATTN_LOOKAHEAD, key_tile, tile), F32),
        ],
        compiler_params=_params("parallel", "parallel", "parallel"),
        name="mla_attention",
    )(qt, k, vt)


def kernel(x, c, positions, cond_w, cond_b, norm_g, ffn_w_gu, ffn_w_down, gla_w_in, gla_w_gate_up, gla_b_gate, gla_g_out, gla_w_out, kv_g_in, kv_cond_w, kv_cond_b, mla_w_kv_a, mla_g_kv, mla_w_kv_b, mla_w_dq, mla_g_q, mla_w_uq, mla_w_out):
    batch, seq, d = x.shape
    depth = cond_w.shape[0]
    n_a = gla_w_in.shape[0]
    d_ff = ffn_w_down.shape[2]
    kd = GLA_HEADS * (d // 2 // GLA_HEADS)
    vd = d

    n_mod = 3 * N_SUBLAYERS
    mods = _cond(c, cond_w, cond_b, cond_w.shape[2] // 4)
    mods = mods.reshape(depth, batch, n_mod, 1, d)
    kv_mods = _cond(c, kv_cond_w[None], kv_cond_b[None], kv_cond_w.shape[1] // 2)
    kv_mods = kv_mods.reshape(batch, 2, 1, d)

    def mod(layer, idx):
        return mods[layer, :, idx]

    def gain(layer, sub, which):
        return norm_g[layer, sub, which].reshape(1, d)

    ffn_w_gu_bf16 = ffn_w_gu.astype(BF16)
    ffn_w_down_bf16 = ffn_w_down.astype(BF16)

    def ffn(x, layer, half, sub, mixer_tail=None):
        return _ffn(x, mod(layer, 3 * sub), mod(layer, 3 * sub + 1), mod(layer, 3 * sub + 2),
                    gain(layer, sub, 0), gain(layer, sub, 1),
                    ffn_w_gu_bf16, ffn_w_down_bf16, layer, half, 0.5, mixer_tail)

    inv_freq = (ROPE_THETA ** (-jnp.arange(0, QK_ROPE, 2, dtype=F32) / QK_ROPE)
                ).reshape(QK_ROPE // 2, 1)
    pos = positions.reshape(batch, 1, seq)

    shared = None
    for layer in range(depth):
        x = ffn(x, layer, 0, 0)
        mixer_tail = None
        if layer < n_a:
            i = layer
            w_in = gla_w_in[i].astype(BF16)
            w_low = jnp.zeros((d, LANES), BF16).at[:, :GLA_GATE_RANK].set(
                w_in[:, 2 * kd + vd:2 * kd + vd + GLA_GATE_RANK])
            w_up = jnp.zeros((LANES, kd), BF16).at[:GLA_GATE_RANK].set(
                gla_w_gate_up[i].astype(BF16))
            x = _gla(x, mod(layer, 3), mod(layer, 4), mod(layer, 5),
                     gain(layer, 1, 0), gain(layer, 1, 1),
                     w_in[:, :kd], w_in[:, kd:2 * kd], w_in[:, 2 * kd:2 * kd + vd],
                     w_low, w_in[:, 2 * kd + vd + GLA_GATE_RANK:], w_up,
                     gla_b_gate[i].reshape(1, kd), gla_g_out[i].reshape(1, -1),
                     gla_w_out[i].astype(BF16))
        else:
            i = layer - n_a
            k_all, vt_all, cos2t, sin2t = shared
            w_uq = mla_w_uq[i].astype(BF16).reshape(-1, MLA_HEADS, QK_NOPE + QK_ROPE)
            qt = _mla_q(x, mod(layer, 3), mod(layer, 4), cos2t, sin2t,
                        gain(layer, 1, 0), mla_w_dq[i].astype(BF16),
                        mla_g_q[i].reshape(1, -1),
                        w_uq[:, :, :QK_NOPE].reshape(-1, MLA_HEADS * QK_NOPE).T,
                        w_uq[:, :, QK_NOPE:].reshape(-1, MLA_HEADS * QK_ROPE).T)
            at = _attention(qt, k_all, vt_all)
            mixer_tail = (at, mod(layer, 5), gain(layer, 1, 1), mla_w_out[i].astype(BF16))
        x = ffn(x, layer, 1, 2, mixer_tail)
        if layer == n_a - 1:
            w_kv_a = mla_w_kv_a.astype(BF16)
            w_kv_b = mla_w_kv_b.astype(BF16).reshape(KV_LORA, MLA_HEADS, QK_NOPE + V_HEAD)
            shared = _shared_kv(
                x, kv_mods[:, 0], kv_mods[:, 1], pos, inv_freq,
                kv_g_in.reshape(1, d), w_kv_a[:, :KV_LORA], w_kv_a[:, KV_LORA:],
                mla_g_kv.reshape(1, KV_LORA),
                w_kv_b[:, :, :QK_NOPE].reshape(KV_LORA, MLA_HEADS * QK_NOPE),
                w_kv_b[:, :, QK_NOPE:].reshape(KV_LORA, MLA_HEADS * V_HEAD).T)
    return x
```

```python
import functools

import jax
import jax.numpy as jnp
from jax import lax
from jax.experimental import pallas as pl
from jax.experimental.pallas import tpu as pltpu

F32 = jnp.float32
BF16 = jnp.bfloat16

EPS = 1e-6
N_SUBLAYERS = 3
GLA_HEADS = 4
GLA_GATE_RANK = 16
GLA_TAU = 16.0
GLA_CHUNK = 64
GLA_BLOCK = 256
MLA_HEADS = 8
QK_NOPE = 128
QK_ROPE = 64
V_HEAD = 128
KV_LORA = 256
ROPE_THETA = 10000.0

LANES = 128
VMEM_LIMIT_BYTES = 56 * 1024 * 1024

TOKEN_TILE = 512
FFN_TILE = 1024
FFN_ROW_SPLIT = 2
FF_CHUNK = 256
ATTN_TILE = 512
ATTN_KEY_TILE = 256
ATTN_HEAD_GROUP = 4
ATTN_LOOKAHEAD = 2
LOG2E = 1.4426950408889634


def _params(*sem):
    return pltpu.CompilerParams(dimension_semantics=sem,
                                vmem_limit_bytes=VMEM_LIMIT_BYTES)


def _rms(x, g):
    return x * lax.rsqrt(jnp.mean(x * x, axis=-1, keepdims=True) + EPS) * g


def _silu(x):
    return x * jax.nn.sigmoid(x)


def _modulated_norm(x, g, scale, shift):
    return _rms(x, g) * (1.0 + scale) + shift


def _const_spec(shape):
    nd = len(shape)
    return pl.BlockSpec(shape, lambda *_: (0,) * nd)


def _tok_spec(tile, width):
    return pl.BlockSpec((1, tile, width), lambda b, i: (b, i, 0))


def _batch_vec_spec(width):
    return pl.BlockSpec((1, 1, width), lambda b, i: (b, 0, 0))


def _cond_kernel(c_ref, w_ref, b_ref, o_ref):
    c_act = _silu(c_ref[...])
    o_ref[0] = jnp.dot(c_act, w_ref[0], preferred_element_type=F32) + b_ref[0]


def _cond(c, w, b, col_tile):
    n_layers, d, n = w.shape
    batch = c.shape[0]
    return pl.pallas_call(
        _cond_kernel,
        grid=(n_layers, n // col_tile),
        in_specs=[
            pl.BlockSpec((batch, d), lambda l, j: (0, 0)),
            pl.BlockSpec((1, d, col_tile), lambda l, j: (l, 0, j)),
            pl.BlockSpec((1, 1, col_tile), lambda l, j: (l, 0, j)),
        ],
        out_specs=pl.BlockSpec((1, batch, col_tile), lambda l, j: (l, 0, j)),
        out_shape=jax.ShapeDtypeStruct((n_layers, batch, n), F32),
        compiler_params=_params("parallel", "parallel"),
        name="cond",
    )(c, w, b.reshape(n_layers, 1, n))


def _ffn_kernel(x_ref, shift_ref, scale_ref, gate_ref, gpre_ref, gpost_ref,
                wg_ref, wu_ref, wd_ref, *rest, res_weight, mixer_tail, tail_fn, n_tail_in):
    rest = list(rest)
    if mixer_tail:
        at_ref, mgate_ref, mgpost_ref, wout_ref = rest[:4]
        rest = rest[4:]
    tail_in = rest[:n_tail_in]
    o_ref = rest[n_tail_in]
    tail_out = rest[n_tail_in + 1:-1]
    act_ref = rest[-1]
    d_ff = wg_ref.shape[1]
    group = x_ref.shape[1] // FFN_ROW_SPLIT
    for r in range(FFN_ROW_SPLIT):
        rows = slice(r * group, (r + 1) * group)
        x = x_ref[0, rows, :]
        if mixer_tail:
            y = lax.dot_general(at_ref[0, :, rows], wout_ref[...], (((0,), (0,)), ((), ())),
                                preferred_element_type=F32)
            x = x + mgate_ref[0] * _rms(y, mgpost_ref[...])
        o_ref[0, rows, :] = x
        h = _modulated_norm(x, gpre_ref[...], scale_ref[0], shift_ref[0]).astype(BF16)
        for j in range(d_ff // FF_CHUNK):
            cols = slice(j * FF_CHUNK, (j + 1) * FF_CHUNK)
            g = jnp.dot(h, wg_ref[:, cols], preferred_element_type=F32)
            u = jnp.dot(h, wu_ref[:, cols], preferred_element_type=F32)
            act_ref[rows, cols] = (_silu(g) * u).astype(BF16)
        y = jnp.dot(act_ref[rows, :], wd_ref[...], preferred_element_type=F32)
        x_out = o_ref[0, rows, :] + (res_weight * gate_ref[0]) * _rms(y, gpost_ref[...])
        o_ref[0, rows, :] = x_out
        if tail_fn is not None:
            tail_fn(x_out, rows, tail_in, tail_out)


def _ffn(x, shift, scale, gate, g_pre, g_post, w_gu, w_d, layer, half, res_weight,
         mixer_tail=None, tail=None, tile=FFN_TILE):
    batch, seq, d = x.shape
    d_ff = w_d.shape[2]

    def resident(shape, col_block=0):
        return pl.BlockSpec((None, None) + shape, lambda b, i: (layer, half, 0, col_block),
                            pipeline_mode=pl.Buffered(1))

    operands = [x, shift, scale, gate, g_pre, g_post, w_gu, w_gu, w_d]
    in_specs = [
        _tok_spec(tile, d),
        _batch_vec_spec(d), _batch_vec_spec(d), _batch_vec_spec(d),
        _const_spec((1, d)), _const_spec((1, d)),
        resident((d, d_ff), 0), resident((d, d_ff), 1), resident((d_ff, d)),
    ]
    name = "ffn"
    if mixer_tail is not None:
        at, m_gate, m_gpost, w_out = mixer_tail
        operands += [at, m_gate, m_gpost, w_out]
        in_specs += [
            pl.BlockSpec((1, at.shape[1], tile), lambda b, i: (b, 0, i)),
            _batch_vec_spec(d), _const_spec((1, d)),
            pl.BlockSpec(w_out.shape, lambda b, i: (0, 0), pipeline_mode=pl.Buffered(1)),
        ]
        name = "ffn_mixer_tail"
    out_specs = [_tok_spec(tile, d)]
    out_shape = [jax.ShapeDtypeStruct(x.shape, x.dtype)]
    if tail is not None:
        operands += tail["operands"]
        in_specs += tail["in_specs"]
        out_specs += tail["out_specs"]
        out_shape += tail["out_shape"]
        name = tail["name"]
    outs = pl.pallas_call(
        functools.partial(_ffn_kernel, res_weight=res_weight,
                          mixer_tail=mixer_tail is not None,
                          tail_fn=None if tail is None else tail["fn"],
                          n_tail_in=0 if tail is None else len(tail["operands"])),
        grid=(batch, seq // tile),
        in_specs=in_specs,
        out_specs=out_specs,
        out_shape=out_shape,
        scratch_shapes=[pltpu.VMEM((tile, d_ff), BF16)],
        compiler_params=_params("parallel", "parallel"),
        name=name,
    )(*operands)
    return outs[0] if tail is None else outs


def _gla_kernel(x_ref, shift_ref, scale_ref, gate_ref, gpre_ref, gpost_ref,
                wq_ref, wk_ref, wv_ref, wlow_ref, wr_ref, wup_ref, bgate_ref,
                gout_ref, wout_ref, o_ref,
                state_ref, qd_ref, kd_ref, kut_ref, v_ref, dect_ref, upd_ref, mix_ref):
    tile = x_ref.shape[1]
    n_chunks = tile // GLA_CHUNK
    n_blocks = tile // GLA_BLOCK
    chunks_per_block = GLA_BLOCK // GLA_CHUNK
    kd_all = wq_ref.shape[1]
    dk = kd_all // GLA_HEADS
    dv = wv_ref.shape[1] // GLA_HEADS

    @pl.when(pl.program_id(1) == 0)
    def _():
        state_ref[...] = jnp.zeros_like(state_ref)

    x = x_ref[0]
    h = _modulated_norm(x, gpre_ref[...], scale_ref[0], shift_ref[0]).astype(BF16)
    low = jnp.dot(h, wlow_ref[...], preferred_element_type=F32).astype(BF16)
    z = jnp.dot(low, wup_ref[...], preferred_element_type=F32) + bgate_ref[...]
    log_a = (jnp.minimum(z, 0.0) - jnp.log(1.0 + jnp.exp(-jnp.abs(z)))) / GLA_TAU
    q = jnp.dot(h, wq_ref[...], preferred_element_type=F32) * dk ** -0.5
    k = jnp.dot(h, wk_ref[...], preferred_element_type=F32)

    row = lax.broadcasted_iota(jnp.int32, (GLA_BLOCK, GLA_BLOCK), 0)
    col = lax.broadcasted_iota(jnp.int32, (GLA_BLOCK, GLA_BLOCK), 1)
    chunk_causal = (col <= row) & (col // GLA_CHUNK == row // GLA_CHUNK)

    tri = jnp.where(chunk_causal, 1.0, 0.0).astype(BF16)
    la_hi = log_a.astype(BF16)
    la_lo = (log_a - la_hi.astype(F32)).astype(BF16)
    b = jnp.concatenate([
        jnp.dot(tri, la_hi[i * GLA_BLOCK:(i + 1) * GLA_BLOCK], preferred_element_type=F32) +
        jnp.dot(tri, la_lo[i * GLA_BLOCK:(i + 1) * GLA_BLOCK], preferred_element_type=F32)
        for i in range(n_blocks)], axis=0)
    v_ref[...] = jnp.dot(h, wv_ref[...], preferred_element_type=F32).astype(BF16)
    r = jnp.dot(h, wr_ref[...], preferred_element_type=F32)
    b3 = b.reshape(n_chunks, GLA_CHUNK, kd_all)
    b_last = b3[:, GLA_CHUNK - 1:, :]
    qd_ref[...] = (q * jnp.exp(b)).astype(BF16)
    kd_ref[...] = (k * jnp.exp(-b)).astype(BF16)
    k_upd = (k.reshape(n_chunks, GLA_CHUNK, kd_all) * jnp.exp(b_last - b3)).reshape(tile, kd_all)
    kut_ref[...] = k_upd.T.astype(BF16)
    dec = jnp.exp(b_last).reshape(n_chunks, kd_all)
    dec = jnp.concatenate([dec, jnp.zeros((LANES - n_chunks, kd_all), F32)], axis=0)
    dect_ref[...] = dec.T

    lane_chunk = lax.broadcasted_iota(jnp.int32, (dk, GLA_BLOCK), 1) // GLA_CHUNK
    nt = (((1,), (1,)), ((), ()))

    steps = [(hd, blk) for blk in range(n_blocks) for hd in range(GLA_HEADS)]

    def attention_block(hd, blk):
        rows = slice(blk * GLA_BLOCK, (blk + 1) * GLA_BLOCK)
        kcols = slice(hd * dk, (hd + 1) * dk)
        return lax.dot_general(qd_ref[rows, kcols], kd_ref[rows, kcols], nt,
                               preferred_element_type=F32)

    def value_block(hd, blk, att):
        rows = slice(blk * GLA_BLOCK, (blk + 1) * GLA_BLOCK)
        vcols = slice(hd * dv, (hd + 1) * dv)
        kut = kut_ref[hd * dk:(hd + 1) * dk, rows]
        lhs = [jnp.where(chunk_causal, att, 0.0).astype(BF16)]
        for n in range(chunks_per_block):
            lhs.append(jnp.where(lane_chunk == n, kut, jnp.zeros_like(kut)))
        res = jnp.dot(jnp.concatenate(lhs, axis=0), v_ref[rows, vcols],
                      preferred_element_type=F32)
        mix_ref[rows, vcols] = res[:GLA_BLOCK]
        for n in range(chunks_per_block):
            upd_ref[hd, blk * chunks_per_block + n] = res[GLA_BLOCK + n * dk:
                                                         GLA_BLOCK + (n + 1) * dk]

    pending = attention_block(*steps[0])
    for idx, st in enumerate(steps):
        nxt = attention_block(*steps[idx + 1]) if idx + 1 < len(steps) else None
        value_block(*st, pending)
        pending = nxt

    for n in range(n_chunks):
        rows = slice(n * GLA_CHUNK, (n + 1) * GLA_CHUNK)
        for hd in range(GLA_HEADS):
            kcols = slice(hd * dk, (hd + 1) * dk)
            vcols = slice(hd * dv, (hd + 1) * dv)
            state = state_ref[hd]
            mix_ref[rows, vcols] += jnp.dot(qd_ref[rows, kcols], state.astype(BF16),
                                            preferred_element_type=F32)
            state_ref[hd] = dect_ref[kcols, n:n + 1] * state + upd_ref[hd, n]

    for hd in range(GLA_HEADS):
        vcols = slice(hd * dv, (hd + 1) * dv)
        mix_ref[:, vcols] = _rms(mix_ref[:, vcols], gout_ref[...])
    y = (mix_ref[...] * _silu(r)).astype(BF16)
    y = jnp.dot(y, wout_ref[...], preferred_element_type=F32)
    o_ref[0] = x + gate_ref[0] * _rms(y, gpost_ref[...])


def _gla(x, shift, scale, gate, g_pre, g_post, w_q, w_k, w_v, w_low, w_r, w_up,
         b_gate, g_out, w_out):
    batch, seq, d = x.shape
    tile = TOKEN_TILE
    kd_all = w_q.shape[1]
    vd_all = w_v.shape[1]
    dk = kd_all // GLA_HEADS
    dv = vd_all // GLA_HEADS
    weights = (w_q, w_k, w_v, w_low, w_r, w_up, b_gate, g_out, w_out)
    return pl.pallas_call(
        _gla_kernel,
        grid=(batch, seq // tile),
        in_specs=[
            _tok_spec(tile, d),
            _batch_vec_spec(d), _batch_vec_spec(d), _batch_vec_spec(d),
            _const_spec((1, d)), _const_spec((1, d)),
        ] + [_const_spec(w.shape) for w in weights],
        out_specs=_tok_spec(tile, d),
        out_shape=jax.ShapeDtypeStruct(x.shape, x.dtype),
        scratch_shapes=[
            pltpu.VMEM((GLA_HEADS, dk, dv), F32),
            pltpu.VMEM((tile, kd_all), BF16),
            pltpu.VMEM((tile, kd_all), BF16),
            pltpu.VMEM((kd_all, tile), BF16),
            pltpu.VMEM((tile, vd_all), BF16),
            pltpu.VMEM((kd_all, LANES), F32),
            pltpu.VMEM((GLA_HEADS, tile // GLA_CHUNK, dk, dv), F32),
            pltpu.VMEM((tile, vd_all), F32),
        ],
        compiler_params=_params("parallel", "arbitrary"),
        name="gla",
    )(x, shift, scale, gate, g_pre, g_post, *weights)


def _swap_halves(x):
    half = x.shape[-1] // 2
    return jnp.concatenate([x[:, half:], x[:, :half]], axis=-1)


def _rope(x, cos2, sin2):
    return x * cos2 + _swap_halves(x) * sin2


def _kv_tail(x, rows, in_refs, out_refs):
    (shift_ref, scale_ref, pos_ref, freq_ref, gin_ref,
     wc_ref, wpe_ref, gkv_ref, wbk_ref, wbvt_ref) = in_refs
    k_ref, vt_ref, cos_ref, sin_ref = out_refs
    h = _modulated_norm(x, gin_ref[...], scale_ref[0], shift_ref[0]).astype(BF16)
    c_kv = jnp.dot(h, wc_ref[...], preferred_element_type=F32)
    k_pe = jnp.dot(h, wpe_ref[...], preferred_element_type=F32)
    c_kv = _rms(c_kv, gkv_ref[...]).astype(BF16)
    k_nope = jnp.dot(c_kv, wbk_ref[...], preferred_element_type=F32)
    vt = lax.dot_general(wbvt_ref[...], c_kv, (((1,), (1,)), ((), ())),
                         preferred_element_type=F32).astype(BF16)
    n_rows = rows.stop - rows.start
    first_key_tile = rows.start // ATTN_KEY_TILE
    for hd in range(MLA_HEADS):
        for r in range(n_rows // ATTN_KEY_TILE):
            vt_ref[0, hd, first_key_tile + r] = vt[hd * V_HEAD:(hd + 1) * V_HEAD,
                                                   r * ATTN_KEY_TILE:(r + 1) * ATTN_KEY_TILE]

    ang = freq_ref[...] * pos_ref[0, :, rows].astype(F32)
    cos = jnp.cos(ang)
    sin = jnp.sin(ang)
    cos2t = jnp.concatenate([cos, cos], axis=0)
    sin2t = jnp.concatenate([-sin, sin], axis=0)
    cos_ref[0, :, rows] = cos2t
    sin_ref[0, :, rows] = sin2t
    k_rope = _rope(k_pe, cos2t.T, sin2t.T).astype(BF16)
    for hd in range(MLA_HEADS):
        k_ref[0, hd, rows, :QK_NOPE] = k_nope[:, hd * QK_NOPE:(hd + 1) * QK_NOPE].astype(BF16)
        k_ref[0, hd, rows, QK_NOPE:] = k_rope


def _kv_tail_args(batch, seq, d, tile, shift, scale, pos, inv_freq, g_in, w_c, w_pe, g_kv,
                  w_bk, w_bvt):
    qk = QK_NOPE + QK_ROPE
    weights = (g_in, w_c, w_pe, g_kv, w_bk, w_bvt)

    def transposed_spec(rows):
        return pl.BlockSpec((1, rows, tile), lambda b, i: (b, 0, i))

    return dict(
        fn=_kv_tail,
        operands=[shift, scale, pos, inv_freq, *weights],
        in_specs=[_batch_vec_spec(d), _batch_vec_spec(d), transposed_spec(1),
                  _const_spec(inv_freq.shape)] + [_const_spec(w.shape) for w in weights],
        out_specs=[
            pl.BlockSpec((1, MLA_HEADS, tile, qk), lambda b, i: (b, 0, i, 0)),
            pl.BlockSpec((1, MLA_HEADS, tile // ATTN_KEY_TILE, V_HEAD, ATTN_KEY_TILE),
                         lambda b, i: (b, 0, i, 0, 0)),
            transposed_spec(QK_ROPE), transposed_spec(QK_ROPE),
        ],
        out_shape=[
            jax.ShapeDtypeStruct((batch, MLA_HEADS, seq, qk), BF16),
            jax.ShapeDtypeStruct((batch, MLA_HEADS, seq // ATTN_KEY_TILE, V_HEAD, ATTN_KEY_TILE),
                                 BF16),
            jax.ShapeDtypeStruct((batch, QK_ROPE, seq), F32),
            jax.ShapeDtypeStruct((batch, QK_ROPE, seq), F32),
        ],
        name="ffn_kv_tail",
    )


def _swap_halves_rows(x):
    half = x.shape[0] // 2
    return jnp.concatenate([x[half:], x[:half]], axis=0)


def _q_tail(x, rows, in_refs, out_refs):
    (shift_ref, scale_ref, cos_ref, sin_ref, gpre_ref,
     wdq_ref, gq_ref, wnt_ref, wrt_ref) = in_refs
    (q_ref,) = out_refs
    h = _modulated_norm(x, gpre_ref[...], scale_ref[0], shift_ref[0]).astype(BF16)
    c_q = jnp.dot(h, wdq_ref[...], preferred_element_type=F32)
    c_q = _rms(c_q, gq_ref[...]).astype(BF16)
    q_scale = (QK_NOPE + QK_ROPE) ** -0.5 * LOG2E
    nt = (((1,), (1,)), ((), ()))
    q_nope = lax.dot_general(wnt_ref[...], c_q, nt, preferred_element_type=F32) * q_scale
    q_rope = lax.dot_general(wrt_ref[...], c_q, nt, preferred_element_type=F32) * q_scale
    cos2 = cos_ref[0, :, rows]
    sin2 = sin_ref[0, :, rows]
    for hd in range(MLA_HEADS):
        q_ref[0, hd, :QK_NOPE, rows] = q_nope[hd * QK_NOPE:(hd + 1) * QK_NOPE].astype(BF16)
        qr = q_rope[hd * QK_ROPE:(hd + 1) * QK_ROPE]
        q_ref[0, hd, QK_NOPE:, rows] = (qr * cos2 + _swap_halves_rows(qr) * sin2).astype(BF16)


def _q_tail_args(batch, seq, d, tile, shift, scale, cos2t, sin2t, g_pre, w_dq, g_q, w_nt, w_rt):
    qk = QK_NOPE + QK_ROPE
    weights = (g_pre, w_dq, g_q, w_nt, w_rt)
    rope_spec = pl.BlockSpec((1, QK_ROPE, tile), lambda b, i: (b, 0, i))
    return dict(
        fn=_q_tail,
        operands=[shift, scale, cos2t, sin2t, *weights],
        in_specs=[_batch_vec_spec(d), _batch_vec_spec(d), rope_spec, rope_spec] +
                 [_const_spec(w.shape) for w in weights],
        out_specs=[pl.BlockSpec((1, MLA_HEADS, qk, tile), lambda b, i: (b, 0, 0, i))],
        out_shape=[jax.ShapeDtypeStruct((batch, MLA_HEADS, qk, seq), BF16)],
        name="ffn_q_tail",
    )


def _attn_kernel(qt_ref, k_ref, vt_ref, ot_ref, m_ref, l_ref, acc_ref, ahead_ref):
    group = qt_ref.shape[1]
    tq = qt_ref.shape[3]
    dv = vt_ref.shape[3]
    tk = ATTN_KEY_TILE
    qi = pl.program_id(2)
    neg = jnp.finfo(F32).min

    m_ref[...] = jnp.full_like(m_ref, -jnp.inf)
    l_ref[...] = jnp.zeros_like(l_ref)
    acc_ref[...] = jnp.zeros_like(acc_ref)

    def scores(hd, key_tile, q_start):
        rows = pl.ds(pl.multiple_of(key_tile * tk, tk), tk)
        return jnp.dot(k_ref[0, hd, rows, :], qt_ref[0, hd, :, q_start:],
                       preferred_element_type=F32)

    def update(hd, key_tile, q_start, s, masked):
        qcols = slice(q_start, tq)
        if masked:
            ki = lax.broadcasted_iota(jnp.int32, s.shape, 0)
            qj = lax.broadcasted_iota(jnp.int32, s.shape, 1)
            s = jnp.where(ki <= qj, s, neg)
        m_prev = m_ref[hd, :, qcols]
        m_new = jnp.maximum(m_prev, jnp.max(s, axis=0, keepdims=True))
        alpha = jnp.exp2(m_prev - m_new)
        p = jnp.exp2(s - m_new)
        l_ref[hd, :, qcols] = alpha * l_ref[hd, :, qcols] + jnp.sum(p, axis=0, keepdims=True)
        m_ref[hd, :, qcols] = m_new
        acc_ref[hd, :, qcols] = alpha * acc_ref[hd, :, qcols] + jnp.dot(
            vt_ref[0, hd, key_tile], p.astype(BF16), preferred_element_type=F32)

    tiles_per_span = tq // tk

    def span(first_tile, masked):
        steps = [(hd, first_tile + r, r * tk if masked else 0)
                 for r in range(tiles_per_span) for hd in range(group)]
        pending = [ahead_ref[i] for i in range(ATTN_LOOKAHEAD)]
        for idx, st in enumerate(steps):
            ahead = idx + ATTN_LOOKAHEAD
            if ahead < len(steps):
                pending.append(scores(*steps[ahead]))
            elif not masked:
                ahead_ref[ahead - len(steps)] = scores(
                    ahead - len(steps), first_tile + tiles_per_span, 0)
            update(*st, pending.pop(0), masked)

    for i in range(ATTN_LOOKAHEAD):
        ahead_ref[i] = scores(i, 0, 0)

    def full_span(j, carry):
        span(j * tiles_per_span, False)
        return carry

    lax.fori_loop(0, qi, full_span, 0)
    span(qi * tiles_per_span, True)
    for hd in range(group):
        ot_ref[0, hd * dv:(hd + 1) * dv, :] = (acc_ref[hd] / l_ref[hd]).astype(ot_ref.dtype)


def _attention(qt, k, vt):
    batch, heads, qk, seq = qt.shape
    _, _, n_key_tiles, dv, key_tile = vt.shape
    tile = ATTN_TILE
    group = ATTN_HEAD_GROUP
    return pl.pallas_call(
        _attn_kernel,
        grid=(batch, heads // group, seq // tile),
        in_specs=[
            pl.BlockSpec((1, group, qk, tile), lambda b, g, i: (b, g, 0, i)),
            pl.BlockSpec((1, group, seq, qk), lambda b, g, i: (b, g, 0, 0)),
            pl.BlockSpec((1, group, n_key_tiles, dv, key_tile),
                         lambda b, g, i: (b, g, 0, 0, 0)),
        ],
        out_specs=pl.BlockSpec((1, group * dv, tile), lambda b, g, i: (b, g, i)),
        out_shape=jax.ShapeDtypeStruct((batch, heads * dv, seq), BF16),
        scratch_shapes=[
            pltpu.VMEM((group, 1, tile), F32),
            pltpu.VMEM((group, 1, tile), F32),
            pltpu.VMEM((group, dv, tile), F32),
            pltpu.VMEM((ATTN_LOOKAHEAD, key_tile, tile), F32),
        ],
        compiler_params=_params("parallel", "parallel", "parallel"),
        name="mla_attention",
    )(qt, k, vt)


def kernel(x, c, positions, cond_w, cond_b, norm_g, ffn_w_gu, ffn_w_down, gla_w_in, gla_w_gate_up, gla_b_gate, gla_g_out, gla_w_out, kv_g_in, kv_cond_w, kv_cond_b, mla_w_kv_a, mla_g_kv, mla_w_kv_b, mla_w_dq, mla_g_q, mla_w_uq, mla_w_out):
    batch, seq, d = x.shape
    depth = cond_w.shape[0]
    n_a = gla_w_in.shape[0]
    d_ff = ffn_w_down.shape[2]
    kd = GLA_HEADS * (d // 2 // GLA_HEADS)
    vd = d

    n_mod = 3 * N_SUBLAYERS
    mods = _cond(c, cond_w, cond_b, cond_w.shape[2] // 4)
    mods = mods.reshape(depth, batch, n_mod, 1, d)
    kv_mods = _cond(c, kv_cond_w[None], kv_cond_b[None], kv_cond_w.shape[1] // 2)
    kv_mods = kv_mods.reshape(batch, 2, 1, d)

    def mod(layer, idx):
        return mods[layer, :, idx]

    def gain(layer, sub, which):
        return norm_g[layer, sub, which].reshape(1, d)

    ffn_w_gu_bf16 = ffn_w_gu.astype(BF16)
    ffn_w_down_bf16 = ffn_w_down.astype(BF16)

    def ffn(x, layer, half, sub, mixer_tail=None, tail=None):
        return _ffn(x, mod(layer, 3 * sub), mod(layer, 3 * sub + 1), mod(layer, 3 * sub + 2),
                    gain(layer, sub, 0), gain(layer, sub, 1),
                    ffn_w_gu_bf16, ffn_w_down_bf16, layer, half, 0.5, mixer_tail, tail)

    inv_freq = (ROPE_THETA ** (-jnp.arange(0, QK_ROPE, 2, dtype=F32) / QK_ROPE)
                ).reshape(QK_ROPE // 2, 1)
    pos = positions.reshape(batch, 1, seq)

    shared = None
    for layer in range(depth):
        mixer_tail = None
        if layer < n_a:
            i = layer
            x = ffn(x, layer, 0, 0)
            w_in = gla_w_in[i].astype(BF16)
            w_low = jnp.zeros((d, LANES), BF16).at[:, :GLA_GATE_RANK].set(
                w_in[:, 2 * kd + vd:2 * kd + vd + GLA_GATE_RANK])
            w_up = jnp.zeros((LANES, kd), BF16).at[:GLA_GATE_RANK].set(
                gla_w_gate_up[i].astype(BF16))
            x = _gla(x, mod(layer, 3), mod(layer, 4), mod(layer, 5),
                     gain(layer, 1, 0), gain(layer, 1, 1),
                     w_in[:, :kd], w_in[:, kd:2 * kd], w_in[:, 2 * kd:2 * kd + vd],
                     w_low, w_in[:, 2 * kd + vd + GLA_GATE_RANK:], w_up,
                     gla_b_gate[i].reshape(1, kd), gla_g_out[i].reshape(1, -1),
                     gla_w_out[i].astype(BF16))
        else:
            i = layer - n_a
            k_all, vt_all, cos2t, sin2t = shared
            w_uq = mla_w_uq[i].astype(BF16).reshape(-1, MLA_HEADS, QK_NOPE + QK_ROPE)
            q_tail = _q_tail_args(
                batch, seq, d, FFN_TILE, mod(layer, 3), mod(layer, 4), cos2t, sin2t,
                gain(layer, 1, 0), mla_w_dq[i].astype(BF16), mla_g_q[i].reshape(1, -1),
                w_uq[:, :, :QK_NOPE].reshape(-1, MLA_HEADS * QK_NOPE).T,
                w_uq[:, :, QK_NOPE:].reshape(-1, MLA_HEADS * QK_ROPE).T)
            x, qt = ffn(x, layer, 0, 0, tail=q_tail)
            at = _attention(qt, k_all, vt_all)
            mixer_tail = (at, mod(layer, 5), gain(layer, 1, 1), mla_w_out[i].astype(BF16))
        if layer == n_a - 1:
            w_kv_a = mla_w_kv_a.astype(BF16)
            w_kv_b = mla_w_kv_b.astype(BF16).reshape(KV_LORA, MLA_HEADS, QK_NOPE + V_HEAD)
            kv_tail = _kv_tail_args(
                batch, seq, d, FFN_TILE, kv_mods[:, 0], kv_mods[:, 1], pos, inv_freq,
                kv_g_in.reshape(1, d), w_kv_a[:, :KV_LORA], w_kv_a[:, KV_LORA:],
                mla_g_kv.reshape(1, KV_LORA),
                w_kv_b[:, :, :QK_NOPE].reshape(KV_LORA, MLA_HEADS * QK_NOPE),
                w_kv_b[:, :, QK_NOPE:].reshape(KV_LORA, MLA_HEADS * V_HEAD).T)
            x, *shared = ffn(x, layer, 1, 2, mixer_tail, kv_tail)
        else:
            x = ffn(x, layer, 1, 2, mixer_tail)
    return x
```

```python
import functools

import jax
import jax.numpy as jnp
from jax import lax
from jax.experimental import pallas as pl
from jax.experimental.pallas import tpu as pltpu

F32 = jnp.float32
BF16 = jnp.bfloat16

EPS = 1e-6
N_SUBLAYERS = 3
GLA_HEADS = 4
GLA_GATE_RANK = 16
GLA_TAU = 16.0
GLA_CHUNK = 64
GLA_BLOCK = 256
MLA_HEADS = 8
QK_NOPE = 128
QK_ROPE = 64
V_HEAD = 128
KV_LORA = 256
ROPE_THETA = 10000.0

LANES = 128
VMEM_LIMIT_BYTES = 56 * 1024 * 1024

TOKEN_TILE = 1024
GLA_PROJ_ROWS = 512
FFN_TILE = 1024
FFN_ROW_SPLIT = 2
FF_CHUNK = 256
ATTN_TILE = 512
ATTN_KEY_TILE = 256
ATTN_HEAD_GROUP = 4
ATTN_LOOKAHEAD = 2
LOG2E = 1.4426950408889634


def _params(*sem):
    return pltpu.CompilerParams(dimension_semantics=sem,
                                vmem_limit_bytes=VMEM_LIMIT_BYTES)


def _rms(x, g):
    return x * lax.rsqrt(jnp.mean(x * x, axis=-1, keepdims=True) + EPS) * g


def _silu(x):
    return x * jax.nn.sigmoid(x)


def _modulated_norm(x, g, scale, shift):
    return _rms(x, g) * (1.0 + scale) + shift


def _const_spec(shape):
    nd = len(shape)
    return pl.BlockSpec(shape, lambda *_: (0,) * nd)


def _tok_spec(tile, width):
    return pl.BlockSpec((1, tile, width), lambda b, i: (b, i, 0))


def _batch_vec_spec(width):
    return pl.BlockSpec((1, 1, width), lambda b, i: (b, 0, 0))


def _cond_kernel(c_ref, w_ref, b_ref, o_ref):
    c_act = _silu(c_ref[...])
    o_ref[0] = jnp.dot(c_act, w_ref[0], preferred_element_type=F32) + b_ref[0]


def _cond(c, w, b, col_tile):
    n_layers, d, n = w.shape
    batch = c.shape[0]
    return pl.pallas_call(
        _cond_kernel,
        grid=(n_layers, n // col_tile),
        in_specs=[
            pl.BlockSpec((batch, d), lambda l, j: (0, 0)),
            pl.BlockSpec((1, d, col_tile), lambda l, j: (l, 0, j)),
            pl.BlockSpec((1, 1, col_tile), lambda l, j: (l, 0, j)),
        ],
        out_specs=pl.BlockSpec((1, batch, col_tile), lambda l, j: (l, 0, j)),
        out_shape=jax.ShapeDtypeStruct((n_layers, batch, n), F32),
        compiler_params=_params("parallel", "parallel"),
        name="cond",
    )(c, w, b.reshape(n_layers, 1, n))


def _ffn_kernel(x_ref, shift_ref, scale_ref, gate_ref, gpre_ref, gpost_ref,
                wg_ref, wu_ref, wd_ref, *rest, res_weight, mixer_tail, tail_fn, n_tail_in):
    rest = list(rest)
    if mixer_tail:
        at_ref, mgate_ref, mgpost_ref, wout_ref = rest[:4]
        rest = rest[4:]
    tail_in = rest[:n_tail_in]
    o_ref = rest[n_tail_in]
    tail_out = rest[n_tail_in + 1:-1]
    act_ref = rest[-1]
    d_ff = wg_ref.shape[1]
    group = x_ref.shape[1] // FFN_ROW_SPLIT
    row_groups = [slice(r * group, (r + 1) * group) for r in range(FFN_ROW_SPLIT)]
    for rows in row_groups:
        x = x_ref[0, rows, :]
        if mixer_tail:
            y = lax.dot_general(at_ref[0, :, rows], wout_ref[...], (((0,), (0,)), ((), ())),
                                preferred_element_type=F32)
            x = x + mgate_ref[0] * _rms(y, mgpost_ref[...])
        o_ref[0, rows, :] = x
    for rows in row_groups:
        h = _modulated_norm(o_ref[0, rows, :], gpre_ref[...], scale_ref[0],
                            shift_ref[0]).astype(BF16)
        for j in range(d_ff // FF_CHUNK):
            cols = slice(j * FF_CHUNK, (j + 1) * FF_CHUNK)
            g = jnp.dot(h, wg_ref[:, cols], preferred_element_type=F32)
            u = jnp.dot(h, wu_ref[:, cols], preferred_element_type=F32)
            act_ref[rows, cols] = (_silu(g) * u).astype(BF16)
        y = jnp.dot(act_ref[rows, :], wd_ref[...], preferred_element_type=F32)
        o_ref[0, rows, :] = (o_ref[0, rows, :] +
                             (res_weight * gate_ref[0]) * _rms(y, gpost_ref[...]))
    if tail_fn is not None:
        tails = [tail_fn(o_ref[0, rows, :], rows, tail_in, tail_out) for rows in row_groups]
        while tails:
            tails = [t for t in tails if next(t, True) is None]


def _ffn(x, shift, scale, gate, g_pre, g_post, w_gu, w_d, layer, half, res_weight,
         mixer_tail=None, tail=None, tile=FFN_TILE):
    batch, seq, d = x.shape
    d_ff = w_d.shape[2]

    def resident(shape, col_block=0):
        return pl.BlockSpec((None, None) + shape, lambda b, i: (layer, half, 0, col_block),
                            pipeline_mode=pl.Buffered(1))

    operands = [x, shift, scale, gate, g_pre, g_post, w_gu, w_gu, w_d]
    in_specs = [
        _tok_spec(tile, d),
        _batch_vec_spec(d), _batch_vec_spec(d), _batch_vec_spec(d),
        _const_spec((1, d)), _const_spec((1, d)),
        resident((d, d_ff), 0), resident((d, d_ff), 1), resident((d_ff, d)),
    ]
    name = "ffn"
    if mixer_tail is not None:
        at, m_gate, m_gpost, w_out = mixer_tail
        operands += [at, m_gate, m_gpost, w_out]
        in_specs += [
            pl.BlockSpec((1, at.shape[1], tile), lambda b, i: (b, 0, i)),
            _batch_vec_spec(d), _const_spec((1, d)),
            pl.BlockSpec(w_out.shape, lambda b, i: (0, 0), pipeline_mode=pl.Buffered(1)),
        ]
        name = "ffn_mixer_tail"
    out_specs = [_tok_spec(tile, d)]
    out_shape = [jax.ShapeDtypeStruct(x.shape, x.dtype)]
    if tail is not None:
        operands += tail["operands"]
        in_specs += tail["in_specs"]
        out_specs += tail["out_specs"]
        out_shape += tail["out_shape"]
        name = tail["name"]
    outs = pl.pallas_call(
        functools.partial(_ffn_kernel, res_weight=res_weight,
                          mixer_tail=mixer_tail is not None,
                          tail_fn=None if tail is None else tail["fn"],
                          n_tail_in=0 if tail is None else len(tail["operands"])),
        grid=(batch, seq // tile),
        in_specs=in_specs,
        out_specs=out_specs,
        out_shape=out_shape,
        scratch_shapes=[pltpu.VMEM((tile, d_ff), BF16)],
        compiler_params=_params("parallel", "parallel"),
        name=name,
    )(*operands)
    return outs[0] if tail is None else outs


def _gla_kernel(x_ref, shift_ref, scale_ref, gate_ref, gpre_ref, gpost_ref,
                wq_ref, wk_ref, wv_ref, wlow_ref, wr_ref, wup_ref, bgate_ref,
                gout_ref, wout_ref, o_ref,
                state_ref, qd_ref, kd_ref, kut_ref, v_ref, dect_ref, upd_ref, mix_ref, r_ref):
    tile = x_ref.shape[1]
    n_chunks = tile // GLA_CHUNK
    n_blocks = tile // GLA_BLOCK
    chunks_per_block = GLA_BLOCK // GLA_CHUNK
    kd_all = wq_ref.shape[1]
    dk = kd_all // GLA_HEADS
    dv = wv_ref.shape[1] // GLA_HEADS

    @pl.when(pl.program_id(1) == 0)
    def _():
        state_ref[...] = jnp.zeros_like(state_ref)

    row = lax.broadcasted_iota(jnp.int32, (GLA_BLOCK, GLA_BLOCK), 0)
    col = lax.broadcasted_iota(jnp.int32, (GLA_BLOCK, GLA_BLOCK), 1)
    chunk_causal = (col <= row) & (col // GLA_CHUNK == row // GLA_CHUNK)
    tri = jnp.where(chunk_causal, 1.0, 0.0).astype(BF16)

    part_chunks = GLA_PROJ_ROWS // GLA_CHUNK
    n_parts = tile // GLA_PROJ_ROWS
    decays = [None] * n_parts

    def projection_part(part):
        rows = slice(part * GLA_PROJ_ROWS, (part + 1) * GLA_PROJ_ROWS)
        h = _modulated_norm(x_ref[0, rows, :], gpre_ref[...], scale_ref[0],
                            shift_ref[0]).astype(BF16)
        low = jnp.dot(h, wlow_ref[...], preferred_element_type=F32).astype(BF16)
        z = jnp.dot(low, wup_ref[...], preferred_element_type=F32) + bgate_ref[...]
        yield
        log_a = (jnp.minimum(z, 0.0) - jnp.log(1.0 + jnp.exp(-jnp.abs(z)))) / GLA_TAU
        q = jnp.dot(h, wq_ref[...], preferred_element_type=F32) * dk ** -0.5
        k = jnp.dot(h, wk_ref[...], preferred_element_type=F32)
        yield

        la_hi = log_a.astype(BF16)
        la_lo = (log_a - la_hi.astype(F32)).astype(BF16)
        b = jnp.concatenate([
            jnp.dot(tri, la_hi[i * GLA_BLOCK:(i + 1) * GLA_BLOCK], preferred_element_type=F32) +
            jnp.dot(tri, la_lo[i * GLA_BLOCK:(i + 1) * GLA_BLOCK], preferred_element_type=F32)
            for i in range(GLA_PROJ_ROWS // GLA_BLOCK)], axis=0)
        v_ref[rows, :] = jnp.dot(h, wv_ref[...], preferred_element_type=F32).astype(BF16)
        r_ref[rows, :] = jnp.dot(h, wr_ref[...], preferred_element_type=F32)
        yield
        b3 = b.reshape(part_chunks, GLA_CHUNK, kd_all)
        b_last = b3[:, GLA_CHUNK - 1:, :]
        qd_ref[rows, :] = (q * jnp.exp(b)).astype(BF16)
        kd_ref[rows, :] = (k * jnp.exp(-b)).astype(BF16)
        k_upd = (k.reshape(part_chunks, GLA_CHUNK, kd_all) *
                 jnp.exp(b_last - b3)).reshape(GLA_PROJ_ROWS, kd_all)
        kut_ref[:, rows] = k_upd.T.astype(BF16)
        decays[part] = jnp.exp(b_last).reshape(part_chunks, kd_all)

    parts = [projection_part(part) for part in range(n_parts)]
    live = set(range(n_parts))
    turn = 0
    while live:
        for part in sorted(live):
            if turn >= part and next(parts[part], True) is not None:
                live.discard(part)
        turn += 1
    dec = jnp.concatenate(decays + [jnp.zeros((LANES - n_chunks, kd_all), F32)], axis=0)
    dect_ref[...] = dec.T

    lane_chunk = lax.broadcasted_iota(jnp.int32, (dk, GLA_BLOCK), 1) // GLA_CHUNK
    nt = (((1,), (1,)), ((), ()))

    steps = [(hd, blk) for blk in range(n_blocks) for hd in range(GLA_HEADS)]

    def attention_block(hd, blk):
        rows = slice(blk * GLA_BLOCK, (blk + 1) * GLA_BLOCK)
        kcols = slice(hd * dk, (hd + 1) * dk)
        return lax.dot_general(qd_ref[rows, kcols], kd_ref[rows, kcols], nt,
                               preferred_element_type=F32)

    def value_block(hd, blk, att):
        rows = slice(blk * GLA_BLOCK, (blk + 1) * GLA_BLOCK)
        vcols = slice(hd * dv, (hd + 1) * dv)
        kut = kut_ref[hd * dk:(hd + 1) * dk, rows]
        lhs = [jnp.where(chunk_causal, att, 0.0).astype(BF16)]
        for n in range(chunks_per_block):
            lhs.append(jnp.where(lane_chunk == n, kut, jnp.zeros_like(kut)))
        res = jnp.dot(jnp.concatenate(lhs, axis=0), v_ref[rows, vcols],
                      preferred_element_type=F32)
        mix_ref[rows, vcols] = res[:GLA_BLOCK]
        for n in range(chunks_per_block):
            upd_ref[hd, blk * chunks_per_block + n] = res[GLA_BLOCK + n * dk:
                                                         GLA_BLOCK + (n + 1) * dk]

    pending = attention_block(*steps[0])
    for idx, st in enumerate(steps):
        nxt = attention_block(*steps[idx + 1]) if idx + 1 < len(steps) else None
        value_block(*st, pending)
        pending = nxt

    for n in range(n_chunks):
        rows = slice(n * GLA_CHUNK, (n + 1) * GLA_CHUNK)
        for hd in range(GLA_HEADS):
            kcols = slice(hd * dk, (hd + 1) * dk)
            vcols = slice(hd * dv, (hd + 1) * dv)
            state = state_ref[hd]
            mix_ref[rows, vcols] += jnp.dot(qd_ref[rows, kcols], state.astype(BF16),
                                            preferred_element_type=F32)
            state_ref[hd] = dect_ref[kcols, n:n + 1] * state + upd_ref[hd, n]

    for hd in range(GLA_HEADS):
        vcols = slice(hd * dv, (hd + 1) * dv)
        mix_ref[:, vcols] = _rms(mix_ref[:, vcols], gout_ref[...])
    y = (mix_ref[...] * _silu(r_ref[...])).astype(BF16)
    y = jnp.dot(y, wout_ref[...], preferred_element_type=F32)
    o_ref[0] = x_ref[0] + gate_ref[0] * _rms(y, gpost_ref[...])


def _gla(x, shift, scale, gate, g_pre, g_post, w_q, w_k, w_v, w_low, w_r, w_up,
         b_gate, g_out, w_out):
    batch, seq, d = x.shape
    tile = TOKEN_TILE
    kd_all = w_q.shape[1]
    vd_all = w_v.shape[1]
    dk = kd_all // GLA_HEADS
    dv = vd_all // GLA_HEADS
    weights = (w_q, w_k, w_v, w_low, w_r, w_up, b_gate, g_out, w_out)
    return pl.pallas_call(
        _gla_kernel,
        grid=(batch, seq // tile),
        in_specs=[
            _tok_spec(tile, d),
            _batch_vec_spec(d), _batch_vec_spec(d), _batch_vec_spec(d),
            _const_spec((1, d)), _const_spec((1, d)),
        ] + [pl.BlockSpec(w.shape, lambda b, i: (0, 0), pipeline_mode=pl.Buffered(1))
             for w in weights],
        out_specs=_tok_spec(tile, d),
        out_shape=jax.ShapeDtypeStruct(x.shape, x.dtype),
        scratch_shapes=[
            pltpu.VMEM((GLA_HEADS, dk, dv), F32),
            pltpu.VMEM((tile, kd_all), BF16),
            pltpu.VMEM((tile, kd_all), BF16),
            pltpu.VMEM((kd_all, tile), BF16),
            pltpu.VMEM((tile, vd_all), BF16),
            pltpu.VMEM((kd_all, LANES), F32),
            pltpu.VMEM((GLA_HEADS, tile // GLA_CHUNK, dk, dv), F32),
            pltpu.VMEM((tile, vd_all), F32),
            pltpu.VMEM((tile, vd_all), F32),
        ],
        compiler_params=_params("parallel", "arbitrary"),
        name="gla",
    )(x, shift, scale, gate, g_pre, g_post, *weights)


def _swap_halves(x):
    half = x.shape[-1] // 2
    return jnp.concatenate([x[:, half:], x[:, :half]], axis=-1)


def _rope(x, cos2, sin2):
    return x * cos2 + _swap_halves(x) * sin2


def _kv_tail(x, rows, in_refs, out_refs):
    (shift_ref, scale_ref, pos_ref, freq_ref, gin_ref,
     wc_ref, wpe_ref, gkv_ref, wbk_ref, wbvt_ref) = in_refs
    k_ref, vt_ref, cos_ref, sin_ref = out_refs
    h = _modulated_norm(x, gin_ref[...], scale_ref[0], shift_ref[0]).astype(BF16)
    c_kv = jnp.dot(h, wc_ref[...], preferred_element_type=F32)
    k_pe = jnp.dot(h, wpe_ref[...], preferred_element_type=F32)
    yield
    c_kv = _rms(c_kv, gkv_ref[...]).astype(BF16)
    k_nope = jnp.dot(c_kv, wbk_ref[...], preferred_element_type=F32)
    vt = lax.dot_general(wbvt_ref[...], c_kv, (((1,), (1,)), ((), ())),
                         preferred_element_type=F32).astype(BF16)
    yield
    n_rows = rows.stop - rows.start
    first_key_tile = rows.start // ATTN_KEY_TILE
    for hd in range(MLA_HEADS):
        for r in range(n_rows // ATTN_KEY_TILE):
            vt_ref[0, hd, first_key_tile + r] = vt[hd * V_HEAD:(hd + 1) * V_HEAD,
                                                   r * ATTN_KEY_TILE:(r + 1) * ATTN_KEY_TILE]

    ang = freq_ref[...] * pos_ref[0, :, rows].astype(F32)
    cos = jnp.cos(ang)
    sin = jnp.sin(ang)
    cos2t = jnp.concatenate([cos, cos], axis=0)
    sin2t = jnp.concatenate([-sin, sin], axis=0)
    cos_ref[0, :, rows] = cos2t
    sin_ref[0, :, rows] = sin2t
    k_rope = _rope(k_pe, cos2t.T, sin2t.T).astype(BF16)
    for hd in range(MLA_HEADS):
        k_ref[0, hd, rows, :QK_NOPE] = k_nope[:, hd * QK_NOPE:(hd + 1) * QK_NOPE].astype(BF16)
        k_ref[0, hd, rows, QK_NOPE:] = k_rope


def _kv_tail_args(batch, seq, d, tile, shift, scale, pos, inv_freq, g_in, w_c, w_pe, g_kv,
                  w_bk, w_bvt):
    qk = QK_NOPE + QK_ROPE
    weights = (g_in, w_c, w_pe, g_kv, w_bk, w_bvt)

    def transposed_spec(rows):
        return pl.BlockSpec((1, rows, tile), lambda b, i: (b, 0, i))

    return dict(
        fn=_kv_tail,
        operands=[shift, scale, pos, inv_freq, *weights],
        in_specs=[_batch_vec_spec(d), _batch_vec_spec(d), transposed_spec(1),
                  _const_spec(inv_freq.shape)] + [_const_spec(w.shape) for w in weights],
        out_specs=[
            pl.BlockSpec((1, MLA_HEADS, tile, qk), lambda b, i: (b, 0, i, 0)),
            pl.BlockSpec((1, MLA_HEADS, tile // ATTN_KEY_TILE, V_HEAD, ATTN_KEY_TILE),
                         lambda b, i: (b, 0, i, 0, 0)),
            transposed_spec(QK_ROPE), transposed_spec(QK_ROPE),
        ],
        out_shape=[
            jax.ShapeDtypeStruct((batch, MLA_HEADS, seq, qk), BF16),
            jax.ShapeDtypeStruct((batch, MLA_HEADS, seq // ATTN_KEY_TILE, V_HEAD, ATTN_KEY_TILE),
                                 BF16),
            jax.ShapeDtypeStruct((batch, QK_ROPE, seq), F32),
            jax.ShapeDtypeStruct((batch, QK_ROPE, seq), F32),
        ],
        name="ffn_kv_tail",
    )


def _swap_halves_rows(x):
    half = x.shape[0] // 2
    return jnp.concatenate([x[half:], x[:half]], axis=0)


def _q_tail(x, rows, in_refs, out_refs):
    (shift_ref, scale_ref, cos_ref, sin_ref, gpre_ref,
     wdq_ref, gq_ref, wnt_ref, wrt_ref) = in_refs
    (q_ref,) = out_refs
    h = _modulated_norm(x, gpre_ref[...], scale_ref[0], shift_ref[0]).astype(BF16)
    c_q = jnp.dot(h, wdq_ref[...], preferred_element_type=F32)
    yield
    c_q = _rms(c_q, gq_ref[...]).astype(BF16)
    q_scale = (QK_NOPE + QK_ROPE) ** -0.5 * LOG2E
    nt = (((1,), (1,)), ((), ()))
    q_nope = lax.dot_general(wnt_ref[...], c_q, nt, preferred_element_type=F32) * q_scale
    q_rope = lax.dot_general(wrt_ref[...], c_q, nt, preferred_element_type=F32) * q_scale
    yield
    cos2 = cos_ref[0, :, rows]
    sin2 = sin_ref[0, :, rows]
    for hd in range(MLA_HEADS):
        q_ref[0, hd, :QK_NOPE, rows] = q_nope[hd * QK_NOPE:(hd + 1) * QK_NOPE].astype(BF16)
        qr = q_rope[hd * QK_ROPE:(hd + 1) * QK_ROPE]
        q_ref[0, hd, QK_NOPE:, rows] = (qr * cos2 + _swap_halves_rows(qr) * sin2).astype(BF16)


def _q_tail_args(batch, seq, d, tile, shift, scale, cos2t, sin2t, g_pre, w_dq, g_q, w_nt, w_rt):
    qk = QK_NOPE + QK_ROPE
    weights = (g_pre, w_dq, g_q, w_nt, w_rt)
    rope_spec = pl.BlockSpec((1, QK_ROPE, tile), lambda b, i: (b, 0, i))
    return dict(
        fn=_q_tail,
        operands=[shift, scale, cos2t, sin2t, *weights],
        in_specs=[_batch_vec_spec(d), _batch_vec_spec(d), rope_spec, rope_spec] +
                 [_const_spec(w.shape) for w in weights],
        out_specs=[pl.BlockSpec((1, MLA_HEADS, qk, tile), lambda b, i: (b, 0, 0, i))],
        out_shape=[jax.ShapeDtypeStruct((batch, MLA_HEADS, qk, seq), BF16)],
        name="ffn_q_tail",
    )


def _attn_kernel(qt_ref, k_ref, vt_ref, ot_ref, m_ref, l_ref, acc_ref, ahead_ref):
    group = qt_ref.shape[1]
    tq = qt_ref.shape[3]
    dv = vt_ref.shape[3]
    tk = ATTN_KEY_TILE
    qi = pl.program_id(2)
    neg = jnp.finfo(F32).min

    m_ref[...] = jnp.full_like(m_ref, -jnp.inf)
    l_ref[...] = jnp.zeros_like(l_ref)
    acc_ref[...] = jnp.zeros_like(acc_ref)

    def scores(hd, key_tile, q_start):
        rows = pl.ds(pl.multiple_of(key_tile * tk, tk), tk)
        return jnp.dot(k_ref[0, hd, rows, :], qt_ref[0, hd, :, q_start:],
                       preferred_element_type=F32)

    def update(hd, key_tile, q_start, s, masked):
        qcols = slice(q_start, tq)
        if masked:
            ki = lax.broadcasted_iota(jnp.int32, s.shape, 0)
            qj = lax.broadcasted_iota(jnp.int32, s.shape, 1)
            s = jnp.where(ki <= qj, s, neg)
        m_prev = m_ref[hd, :, qcols]
        m_new = jnp.maximum(m_prev, jnp.max(s, axis=0, keepdims=True))
        alpha = jnp.exp2(m_prev - m_new)
        p = jnp.exp2(s - m_new)
        l_ref[hd, :, qcols] = alpha * l_ref[hd, :, qcols] + jnp.sum(p, axis=0, keepdims=True)
        m_ref[hd, :, qcols] = m_new
        acc_ref[hd, :, qcols] = alpha * acc_ref[hd, :, qcols] + jnp.dot(
            vt_ref[0, hd, key_tile], p.astype(BF16), preferred_element_type=F32)

    tiles_per_span = tq // tk

    def span(first_tile, masked):
        steps = [(hd, first_tile + r, r * tk if masked else 0)
                 for r in range(tiles_per_span) for hd in range(group)]
        pending = [ahead_ref[i] for i in range(ATTN_LOOKAHEAD)]
        for idx, st in enumerate(steps):
            ahead = idx + ATTN_LOOKAHEAD
            if ahead < len(steps):
                pending.append(scores(*steps[ahead]))
            elif not masked:
                ahead_ref[ahead - len(steps)] = scores(
                    ahead - len(steps), first_tile + tiles_per_span, 0)
            update(*st, pending.pop(0), masked)

    for i in range(ATTN_LOOKAHEAD):
        ahead_ref[i] = scores(i, 0, 0)

    def full_span(j, carry):
        span(j * tiles_per_span, False)
        return carry

    lax.fori_loop(0, qi, full_span, 0)
    span(qi * tiles_per_span, True)
    for hd in range(group):
        ot_ref[0, hd * dv:(hd + 1) * dv, :] = (acc_ref[hd] / l_ref[hd]).astype(ot_ref.dtype)


def _attention(qt, k, vt):
    batch, heads, qk, seq = qt.shape
    _, _, n_key_tiles, dv, key_tile = vt.shape
    tile = ATTN_TILE
    group = ATTN_HEAD_GROUP
    return pl.pallas_call(
        _attn_kernel,
        grid=(batch, heads // group, seq // tile),
        in_specs=[
            pl.BlockSpec((1, group, qk, tile), lambda b, g, i: (b, g, 0, i)),
            pl.BlockSpec((1, group, seq, qk), lambda b, g, i: (b, g, 0, 0)),
            pl.BlockSpec((1, group, n_key_tiles, dv, key_tile),
                         lambda b, g, i: (b, g, 0, 0, 0)),
        ],
        out_specs=pl.BlockSpec((1, group * dv, tile), lambda b, g, i: (b, g, i)),
        out_shape=jax.ShapeDtypeStruct((batch, heads * dv, seq), BF16),
        scratch_shapes=[
            pltpu.VMEM((group, 1, tile), F32),
            pltpu.VMEM((group, 1, tile), F32),
            pltpu.VMEM((group, dv, tile), F32),
            pltpu.VMEM((ATTN_LOOKAHEAD, key_tile, tile), F32),
        ],
        compiler_params=_params("parallel", "parallel", "parallel"),
        name="mla_attention",
    )(qt, k, vt)


def kernel(x, c, positions, cond_w, cond_b, norm_g, ffn_w_gu, ffn_w_down, gla_w_in, gla_w_gate_up, gla_b_gate, gla_g_out, gla_w_out, kv_g_in, kv_cond_w, kv_cond_b, mla_w_kv_a, mla_g_kv, mla_w_kv_b, mla_w_dq, mla_g_q, mla_w_uq, mla_w_out):
    batch, seq, d = x.shape
    depth = cond_w.shape[0]
    n_a = gla_w_in.shape[0]
    d_ff = ffn_w_down.shape[2]
    kd = GLA_HEADS * (d // 2 // GLA_HEADS)
    vd = d

    n_mod = 3 * N_SUBLAYERS
    mods = _cond(c, cond_w, cond_b, cond_w.shape[2] // 4)
    mods = mods.reshape(depth, batch, n_mod, 1, d)
    kv_mods = _cond(c, kv_cond_w[None], kv_cond_b[None], kv_cond_w.shape[1] // 2)
    kv_mods = kv_mods.reshape(batch, 2, 1, d)

    def mod(layer, idx):
        return mods[layer, :, idx]

    def gain(layer, sub, which):
        return norm_g[layer, sub, which].reshape(1, d)

    ffn_w_gu_bf16 = ffn_w_gu.astype(BF16)
    ffn_w_down_bf16 = ffn_w_down.astype(BF16)

    def ffn(x, layer, half, sub, mixer_tail=None, tail=None):
        return _ffn(x, mod(layer, 3 * sub), mod(layer, 3 * sub + 1), mod(layer, 3 * sub + 2),
                    gain(layer, sub, 0), gain(layer, sub, 1),
                    ffn_w_gu_bf16, ffn_w_down_bf16, layer, half, 0.5, mixer_tail, tail)

    inv_freq = (ROPE_THETA ** (-jnp.arange(0, QK_ROPE, 2, dtype=F32) / QK_ROPE)
                ).reshape(QK_ROPE // 2, 1)
    pos = positions.reshape(batch, 1, seq)

    shared = None
    for layer in range(depth):
        mixer_tail = None
        if layer < n_a:
            i = layer
            x = ffn(x, layer, 0, 0)
            w_in = gla_w_in[i].astype(BF16)
            w_low = jnp.zeros((d, LANES), BF16).at[:, :GLA_GATE_RANK].set(
                w_in[:, 2 * kd + vd:2 * kd + vd + GLA_GATE_RANK])
            w_up = jnp.zeros((LANES, kd), BF16).at[:GLA_GATE_RANK].set(
                gla_w_gate_up[i].astype(BF16))
            x = _gla(x, mod(layer, 3), mod(layer, 4), mod(layer, 5),
                     gain(layer, 1, 0), gain(layer, 1, 1),
                     w_in[:, :kd], w_in[:, kd:2 * kd], w_in[:, 2 * kd:2 * kd + vd],
                     w_low, w_in[:, 2 * kd + vd + GLA_GATE_RANK:], w_up,
                     gla_b_gate[i].reshape(1, kd), gla_g_out[i].reshape(1, -1),
                     gla_w_out[i].astype(BF16))
        else:
            i = layer - n_a
            k_all, vt_all, cos2t, sin2t = shared
            w_uq = mla_w_uq[i].astype(BF16).reshape(-1, MLA_HEADS, QK_NOPE + QK_ROPE)
            q_tail = _q_tail_args(
                batch, seq, d, FFN_TILE, mod(layer, 3), mod(layer, 4), cos2t, sin2t,
                gain(layer, 1, 0), mla_w_dq[i].astype(BF16), mla_g_q[i].reshape(1, -1),
                w_uq[:, :, :QK_NOPE].reshape(-1, MLA_HEADS * QK_NOPE).T,
                w_uq[:, :, QK_NOPE:].reshape(-1, MLA_HEADS * QK_ROPE).T)
            x, qt = ffn(x, layer, 0, 0, tail=q_tail)
            at = _attention(qt, k_all, vt_all)
            mixer_tail = (at, mod(layer, 5), gain(layer, 1, 1), mla_w_out[i].astype(BF16))
        if layer == n_a - 1:
            w_kv_a = mla_w_kv_a.astype(BF16)
            w_kv_b = mla_w_kv_b.astype(BF16).reshape(KV_LORA, MLA_HEADS, QK_NOPE + V_HEAD)
            kv_tail = _kv_tail_args(
                batch, seq, d, FFN_TILE, kv_mods[:, 0], kv_mods[:, 1], pos, inv_freq,
                kv_g_in.reshape(1, d), w_kv_a[:, :KV_LORA], w_kv_a[:, KV_LORA:],
                mla_g_kv.reshape(1, KV_LORA),
                w_kv_b[:, :, :QK_NOPE].reshape(KV_LORA, MLA_HEADS * QK_NOPE),
                w_kv_b[:, :, QK_NOPE:].reshape(KV_LORA, MLA_HEADS * V_HEAD).T)
            x, *shared = ffn(x, layer, 1, 2, mixer_tail, kv_tail)
        else:
            x = ffn(x, layer, 1, 2, mixer_tail)
    return x
```

```python
import functools

import jax
import jax.numpy as jnp
from jax import lax
from jax.experimental import pallas as pl
from jax.experimental.pallas import tpu as pltpu

F32 = jnp.float32
BF16 = jnp.bfloat16

EPS = 1e-6
N_SUBLAYERS = 3
GLA_HEADS = 4
GLA_GATE_RANK = 16
GLA_TAU = 16.0
GLA_CHUNK = 64
GLA_BLOCK = 256
MLA_HEADS = 8
QK_NOPE = 128
QK_ROPE = 64
QK_PAD = 256
V_HEAD = 128
KV_LORA = 256
ROPE_THETA = 10000.0

LANES = 128
VMEM_LIMIT_BYTES = 56 * 1024 * 1024

TOKEN_TILE = 1024
GLA_PROJ_ROWS = 512
FFN_TILE = 1024
FFN_ROW_SPLIT = 2
FF_CHUNK = 256
ATTN_TILE = 512
ATTN_KEY_TILE = 256
ATTN_HEAD_GROUP = 4
ATTN_LOOKAHEAD = 2
LOG2E = 1.4426950408889634


def _params(*sem):
    return pltpu.CompilerParams(dimension_semantics=sem,
                                vmem_limit_bytes=VMEM_LIMIT_BYTES)


def _rms(x, g):
    return x * lax.rsqrt(jnp.mean(x * x, axis=-1, keepdims=True) + EPS) * g


def _silu(x):
    return x * jax.nn.sigmoid(x)


def _modulated_norm(x, g, scale, shift):
    return _rms(x, g) * (1.0 + scale) + shift


def _const_spec(shape):
    nd = len(shape)
    return pl.BlockSpec(shape, lambda *_: (0,) * nd)


def _tok_spec(tile, width):
    return pl.BlockSpec((1, tile, width), lambda b, i: (b, i, 0))


def _batch_vec_spec(width):
    return pl.BlockSpec((1, 1, width), lambda b, i: (b, 0, 0))


def _cond_kernel(c_ref, w_ref, b_ref, o_ref):
    c_act = _silu(c_ref[...])
    o_ref[0] = jnp.dot(c_act, w_ref[0], preferred_element_type=F32) + b_ref[0]


def _cond(c, w, b, col_tile):
    n_layers, d, n = w.shape
    batch = c.shape[0]
    return pl.pallas_call(
        _cond_kernel,
        grid=(n_layers, n // col_tile),
        in_specs=[
            pl.BlockSpec((batch, d), lambda l, j: (0, 0)),
            pl.BlockSpec((1, d, col_tile), lambda l, j: (l, 0, j)),
            pl.BlockSpec((1, 1, col_tile), lambda l, j: (l, 0, j)),
        ],
        out_specs=pl.BlockSpec((1, batch, col_tile), lambda l, j: (l, 0, j)),
        out_shape=jax.ShapeDtypeStruct((n_layers, batch, n), F32),
        compiler_params=_params("parallel", "parallel"),
        name="cond",
    )(c, w, b.reshape(n_layers, 1, n))


def _ffn_kernel(x_ref, shift_ref, scale_ref, gate_ref, gpre_ref, gpost_ref,
                wg_ref, wu_ref, wd_ref, *rest, res_weight, mixer_tail, tail_fn, n_tail_in):
    rest = list(rest)
    if mixer_tail:
        at_ref, mgate_ref, mgpost_ref, wout_ref = rest[:4]
        rest = rest[4:]
    tail_in = rest[:n_tail_in]
    o_ref = rest[n_tail_in]
    tail_out = rest[n_tail_in + 1:-1]
    act_ref = rest[-1]
    d_ff = wg_ref.shape[1]
    group = x_ref.shape[1] // FFN_ROW_SPLIT
    row_groups = [slice(r * group, (r + 1) * group) for r in range(FFN_ROW_SPLIT)]
    for rows in row_groups:
        x = x_ref[0, rows, :]
        if mixer_tail:
            y = lax.dot_general(at_ref[0, :, rows], wout_ref[...], (((0,), (0,)), ((), ())),
                                preferred_element_type=F32)
            x = x + mgate_ref[0] * _rms(y, mgpost_ref[...])
        o_ref[0, rows, :] = x
    for rows in row_groups:
        h = _modulated_norm(o_ref[0, rows, :], gpre_ref[...], scale_ref[0],
                            shift_ref[0]).astype(BF16)
        for j in range(d_ff // FF_CHUNK):
            cols = slice(j * FF_CHUNK, (j + 1) * FF_CHUNK)
            g = jnp.dot(h, wg_ref[:, cols], preferred_element_type=F32)
            u = jnp.dot(h, wu_ref[:, cols], preferred_element_type=F32)
            act_ref[rows, cols] = (_silu(g) * u).astype(BF16)
        y = jnp.dot(act_ref[rows, :], wd_ref[...], preferred_element_type=F32)
        o_ref[0, rows, :] = (o_ref[0, rows, :] +
                             (res_weight * gate_ref[0]) * _rms(y, gpost_ref[...]))
    if tail_fn is not None:
        tails = [tail_fn(o_ref[0, rows, :], rows, tail_in, tail_out) for rows in row_groups]
        while tails:
            tails = [t for t in tails if next(t, True) is None]


def _ffn(x, shift, scale, gate, g_pre, g_post, w_gu, w_d, layer, half, res_weight,
         mixer_tail=None, tail=None, tile=FFN_TILE):
    batch, seq, d = x.shape
    d_ff = w_d.shape[2]

    def resident(shape, col_block=0):
        return pl.BlockSpec((None, None) + shape, lambda b, i: (layer, half, 0, col_block),
                            pipeline_mode=pl.Buffered(1))

    operands = [x, shift, scale, gate, g_pre, g_post, w_gu, w_gu, w_d]
    in_specs = [
        _tok_spec(tile, d),
        _batch_vec_spec(d), _batch_vec_spec(d), _batch_vec_spec(d),
        _const_spec((1, d)), _const_spec((1, d)),
        resident((d, d_ff), 0), resident((d, d_ff), 1), resident((d_ff, d)),
    ]
    name = "ffn"
    if mixer_tail is not None:
        at, m_gate, m_gpost, w_out = mixer_tail
        operands += [at, m_gate, m_gpost, w_out]
        in_specs += [
            pl.BlockSpec((1, at.shape[1], tile), lambda b, i: (b, 0, i)),
            _batch_vec_spec(d), _const_spec((1, d)),
            pl.BlockSpec(w_out.shape, lambda b, i: (0, 0), pipeline_mode=pl.Buffered(1)),
        ]
        name = "ffn_mixer_tail"
    out_specs = [_tok_spec(tile, d)]
    out_shape = [jax.ShapeDtypeStruct(x.shape, x.dtype)]
    if tail is not None:
        operands += tail["operands"]
        in_specs += tail["in_specs"]
        out_specs += tail["out_specs"]
        out_shape += tail["out_shape"]
        name = tail["name"]
    outs = pl.pallas_call(
        functools.partial(_ffn_kernel, res_weight=res_weight,
                          mixer_tail=mixer_tail is not None,
                          tail_fn=None if tail is None else tail["fn"],
                          n_tail_in=0 if tail is None else len(tail["operands"])),
        grid=(batch, seq // tile),
        in_specs=in_specs,
        out_specs=out_specs,
        out_shape=out_shape,
        scratch_shapes=[pltpu.VMEM((tile, d_ff), BF16)],
        compiler_params=_params("parallel", "parallel"),
        name=name,
    )(*operands)
    return outs[0] if tail is None else outs


def _gla_kernel(x_ref, shift_ref, scale_ref, gate_ref, gpre_ref, gpost_ref,
                wq_ref, wk_ref, wv_ref, wlow_ref, wr_ref, wup_ref, bgate_ref,
                gout_ref, wout_ref, o_ref,
                state_ref, qd_ref, kd_ref, kut_ref, v_ref, dect_ref, upd_ref, mix_ref, r_ref):
    tile = x_ref.shape[1]
    n_chunks = tile // GLA_CHUNK
    n_blocks = tile // GLA_BLOCK
    chunks_per_block = GLA_BLOCK // GLA_CHUNK
    kd_all = wq_ref.shape[1]
    dk = kd_all // GLA_HEADS
    dv = wv_ref.shape[1] // GLA_HEADS

    @pl.when(pl.program_id(1) == 0)
    def _():
        state_ref[...] = jnp.zeros_like(state_ref)

    row = lax.broadcasted_iota(jnp.int32, (GLA_BLOCK, GLA_BLOCK), 0)
    col = lax.broadcasted_iota(jnp.int32, (GLA_BLOCK, GLA_BLOCK), 1)
    chunk_causal = (col <= row) & (col // GLA_CHUNK == row // GLA_CHUNK)
    tri = jnp.where(chunk_causal, 1.0, 0.0).astype(BF16)

    part_chunks = GLA_PROJ_ROWS // GLA_CHUNK
    n_parts = tile // GLA_PROJ_ROWS
    decays = [None] * n_parts

    def projection_part(part):
        rows = slice(part * GLA_PROJ_ROWS, (part + 1) * GLA_PROJ_ROWS)
        h = _modulated_norm(x_ref[0, rows, :], gpre_ref[...], scale_ref[0],
                            shift_ref[0]).astype(BF16)
        low = jnp.dot(h, wlow_ref[...], preferred_element_type=F32).astype(BF16)
        z = jnp.dot(low, wup_ref[...], preferred_element_type=F32) + bgate_ref[...]
        yield
        log_a = (jnp.minimum(z, 0.0) - jnp.log(1.0 + jnp.exp(-jnp.abs(z)))) / GLA_TAU
        q = jnp.dot(h, wq_ref[...], preferred_element_type=F32) * dk ** -0.5
        k = jnp.dot(h, wk_ref[...], preferred_element_type=F32)
        yield

        la_hi = log_a.astype(BF16)
        la_lo = (log_a - la_hi.astype(F32)).astype(BF16)
        b = jnp.concatenate([
            jnp.dot(tri, la_hi[i * GLA_BLOCK:(i + 1) * GLA_BLOCK], preferred_element_type=F32) +
            jnp.dot(tri, la_lo[i * GLA_BLOCK:(i + 1) * GLA_BLOCK], preferred_element_type=F32)
            for i in range(GLA_PROJ_ROWS // GLA_BLOCK)], axis=0)
        v_ref[rows, :] = jnp.dot(h, wv_ref[...], preferred_element_type=F32).astype(BF16)
        r_ref[rows, :] = jnp.dot(h, wr_ref[...], preferred_element_type=F32)
        yield
        b3 = b.reshape(part_chunks, GLA_CHUNK, kd_all)
        b_last = b3[:, GLA_CHUNK - 1:, :]
        qd_ref[rows, :] = (q * jnp.exp(b)).astype(BF16)
        kd_ref[rows, :] = (k * jnp.exp(-b)).astype(BF16)
        k_upd = (k.reshape(part_chunks, GLA_CHUNK, kd_all) *
                 jnp.exp(b_last - b3)).reshape(GLA_PROJ_ROWS, kd_all)
        kut_ref[:, rows] = k_upd.T.astype(BF16)
        decays[part] = jnp.exp(b_last).reshape(part_chunks, kd_all)

    parts = [projection_part(part) for part in range(n_parts)]
    live = set(range(n_parts))
    turn = 0
    while live:
        for part in sorted(live):
            if turn >= part and next(parts[part], True) is not None:
                live.discard(part)
        turn += 1
    dec = jnp.concatenate(decays + [jnp.zeros((LANES - n_chunks, kd_all), F32)], axis=0)
    dect_ref[...] = dec.T

    lane_chunk = lax.broadcasted_iota(jnp.int32, (dk, GLA_BLOCK), 1) // GLA_CHUNK
    nt = (((1,), (1,)), ((), ()))

    steps = [(hd, blk) for blk in range(n_blocks) for hd in range(GLA_HEADS)]

    def attention_block(hd, blk):
        rows = slice(blk * GLA_BLOCK, (blk + 1) * GLA_BLOCK)
        kcols = slice(hd * dk, (hd + 1) * dk)
        return lax.dot_general(qd_ref[rows, kcols], kd_ref[rows, kcols], nt,
                               preferred_element_type=F32)

    def value_block(hd, blk, att):
        rows = slice(blk * GLA_BLOCK, (blk + 1) * GLA_BLOCK)
        vcols = slice(hd * dv, (hd + 1) * dv)
        kut = kut_ref[hd * dk:(hd + 1) * dk, rows]
        lhs = [jnp.where(chunk_causal, att, 0.0).astype(BF16)]
        for n in range(chunks_per_block):
            lhs.append(jnp.where(lane_chunk == n, kut, jnp.zeros_like(kut)))
        res = jnp.dot(jnp.concatenate(lhs, axis=0), v_ref[rows, vcols],
                      preferred_element_type=F32)
        mix_ref[rows, vcols] = res[:GLA_BLOCK]
        for n in range(chunks_per_block):
            upd_ref[hd, blk * chunks_per_block + n] = res[GLA_BLOCK + n * dk:
                                                         GLA_BLOCK + (n + 1) * dk]

    pending = attention_block(*steps[0])
    for idx, st in enumerate(steps):
        nxt = attention_block(*steps[idx + 1]) if idx + 1 < len(steps) else None
        value_block(*st, pending)
        pending = nxt

    for n in range(n_chunks):
        rows = slice(n * GLA_CHUNK, (n + 1) * GLA_CHUNK)
        for hd in range(GLA_HEADS):
            kcols = slice(hd * dk, (hd + 1) * dk)
            vcols = slice(hd * dv, (hd + 1) * dv)
            state = state_ref[hd]
            mix_ref[rows, vcols] += jnp.dot(qd_ref[rows, kcols], state.astype(BF16),
                                            preferred_element_type=F32)
            state_ref[hd] = dect_ref[kcols, n:n + 1] * state + upd_ref[hd, n]

    for hd in range(GLA_HEADS):
        vcols = slice(hd * dv, (hd + 1) * dv)
        mix_ref[:, vcols] = _rms(mix_ref[:, vcols], gout_ref[...])
    y = (mix_ref[...] * _silu(r_ref[...])).astype(BF16)
    y = jnp.dot(y, wout_ref[...], preferred_element_type=F32)
    o_ref[0] = x_ref[0] + gate_ref[0] * _rms(y, gpost_ref[...])


def _gla(x, shift, scale, gate, g_pre, g_post, w_q, w_k, w_v, w_low, w_r, w_up,
         b_gate, g_out, w_out):
    batch, seq, d = x.shape
    tile = TOKEN_TILE
    kd_all = w_q.shape[1]
    vd_all = w_v.shape[1]
    dk = kd_all // GLA_HEADS
    dv = vd_all // GLA_HEADS
    weights = (w_q, w_k, w_v, w_low, w_r, w_up, b_gate, g_out, w_out)
    return pl.pallas_call(
        _gla_kernel,
        grid=(batch, seq // tile),
        in_specs=[
            _tok_spec(tile, d),
            _batch_vec_spec(d), _batch_vec_spec(d), _batch_vec_spec(d),
            _const_spec((1, d)), _const_spec((1, d)),
        ] + [pl.BlockSpec(w.shape, lambda b, i: (0, 0), pipeline_mode=pl.Buffered(1))
             for w in weights],
        out_specs=_tok_spec(tile, d),
        out_shape=jax.ShapeDtypeStruct(x.shape, x.dtype),
        scratch_shapes=[
            pltpu.VMEM((GLA_HEADS, dk, dv), F32),
            pltpu.VMEM((tile, kd_all), BF16),
            pltpu.VMEM((tile, kd_all), BF16),
            pltpu.VMEM((kd_all, tile), BF16),
            pltpu.VMEM((tile, vd_all), BF16),
            pltpu.VMEM((kd_all, LANES), F32),
            pltpu.VMEM((GLA_HEADS, tile // GLA_CHUNK, dk, dv), F32),
            pltpu.VMEM((tile, vd_all), F32),
            pltpu.VMEM((tile, vd_all), F32),
        ],
        compiler_params=_params("parallel", "arbitrary"),
        name="gla",
    )(x, shift, scale, gate, g_pre, g_post, *weights)


def _swap_halves(x):
    half = x.shape[-1] // 2
    return jnp.concatenate([x[:, half:], x[:, :half]], axis=-1)


def _rope(x, cos2, sin2):
    return x * cos2 + _swap_halves(x) * sin2


def _kv_tail(x, rows, in_refs, out_refs):
    (shift_ref, scale_ref, pos_ref, freq_ref, gin_ref,
     wc_ref, wpe_ref, gkv_ref, wbk_ref, wbvt_ref) = in_refs
    k_ref, vt_ref, cos_ref, sin_ref = out_refs
    h = _modulated_norm(x, gin_ref[...], scale_ref[0], shift_ref[0]).astype(BF16)
    c_kv = jnp.dot(h, wc_ref[...], preferred_element_type=F32)
    k_pe = jnp.dot(h, wpe_ref[...], preferred_element_type=F32)
    yield
    c_kv = _rms(c_kv, gkv_ref[...]).astype(BF16)
    k_nope = jnp.dot(c_kv, wbk_ref[...], preferred_element_type=F32)
    vt = lax.dot_general(wbvt_ref[...], c_kv, (((1,), (1,)), ((), ())),
                         preferred_element_type=F32).astype(BF16)
    yield
    n_rows = rows.stop - rows.start
    first_key_tile = rows.start // ATTN_KEY_TILE
    for hd in range(MLA_HEADS):
        for r in range(n_rows // ATTN_KEY_TILE):
            vt_ref[0, hd, first_key_tile + r] = vt[hd * V_HEAD:(hd + 1) * V_HEAD,
                                                   r * ATTN_KEY_TILE:(r + 1) * ATTN_KEY_TILE]

    ang = freq_ref[...] * pos_ref[0, :, rows].astype(F32)
    cos = jnp.cos(ang)
    sin = jnp.sin(ang)
    cos2t = jnp.concatenate([cos, cos], axis=0)
    sin2t = jnp.concatenate([-sin, sin], axis=0)
    cos_ref[0, :, rows] = cos2t
    sin_ref[0, :, rows] = sin2t
    k_rope = _rope(k_pe, cos2t.T, sin2t.T).astype(BF16)
    for hd in range(MLA_HEADS):
        k_ref[0, hd, rows, :QK_NOPE] = k_nope[:, hd * QK_NOPE:(hd + 1) * QK_NOPE].astype(BF16)
        k_ref[0, hd, rows, QK_NOPE:QK_NOPE + QK_ROPE] = k_rope
        k_ref[0, hd, rows, QK_NOPE + QK_ROPE:] = jnp.zeros(
            (rows.stop - rows.start, QK_PAD - QK_NOPE - QK_ROPE), BF16)


def _kv_tail_args(batch, seq, d, tile, shift, scale, pos, inv_freq, g_in, w_c, w_pe, g_kv,
                  w_bk, w_bvt):
    qk = QK_PAD
    weights = (g_in, w_c, w_pe, g_kv, w_bk, w_bvt)

    def transposed_spec(rows):
        return pl.BlockSpec((1, rows, tile), lambda b, i: (b, 0, i))

    return dict(
        fn=_kv_tail,
        operands=[shift, scale, pos, inv_freq, *weights],
        in_specs=[_batch_vec_spec(d), _batch_vec_spec(d), transposed_spec(1),
                  _const_spec(inv_freq.shape)] + [_const_spec(w.shape) for w in weights],
        out_specs=[
            pl.BlockSpec((1, MLA_HEADS, tile, qk), lambda b, i: (b, 0, i, 0)),
            pl.BlockSpec((1, MLA_HEADS, tile // ATTN_KEY_TILE, V_HEAD, ATTN_KEY_TILE),
                         lambda b, i: (b, 0, i, 0, 0)),
            transposed_spec(QK_ROPE), transposed_spec(QK_ROPE),
        ],
        out_shape=[
            jax.ShapeDtypeStruct((batch, MLA_HEADS, seq, qk), BF16),
            jax.ShapeDtypeStruct((batch, MLA_HEADS, seq // ATTN_KEY_TILE, V_HEAD, ATTN_KEY_TILE),
                                 BF16),
            jax.ShapeDtypeStruct((batch, QK_ROPE, seq), F32),
            jax.ShapeDtypeStruct((batch, QK_ROPE, seq), F32),
        ],
        name="ffn_kv_tail",
    )


def _swap_halves_rows(x):
    half = x.shape[0] // 2
    return jnp.concatenate([x[half:], x[:half]], axis=0)


def _q_tail(x, rows, in_refs, out_refs):
    (shift_ref, scale_ref, cos_ref, sin_ref, gpre_ref,
     wdq_ref, gq_ref, wnt_ref, wrt_ref) = in_refs
    (q_ref,) = out_refs
    h = _modulated_norm(x, gpre_ref[...], scale_ref[0], shift_ref[0]).astype(BF16)
    c_q = jnp.dot(h, wdq_ref[...], preferred_element_type=F32)
    yield
    c_q = _rms(c_q, gq_ref[...]).astype(BF16)
    q_scale = (QK_NOPE + QK_ROPE) ** -0.5 * LOG2E
    nt = (((1,), (1,)), ((), ()))
    q_nope = lax.dot_general(wnt_ref[...], c_q, nt, preferred_element_type=F32) * q_scale
    q_rope = lax.dot_general(wrt_ref[...], c_q, nt, preferred_element_type=F32) * q_scale
    yield
    cos2 = cos_ref[0, :, rows]
    sin2 = sin_ref[0, :, rows]
    for hd in range(MLA_HEADS):
        q_ref[0, hd, :QK_NOPE, rows] = q_nope[hd * QK_NOPE:(hd + 1) * QK_NOPE].astype(BF16)
        qr = q_rope[hd * QK_ROPE:(hd + 1) * QK_ROPE]
        q_ref[0, hd, QK_NOPE:QK_NOPE + QK_ROPE, rows] = (
            qr * cos2 + _swap_halves_rows(qr) * sin2).astype(BF16)
        q_ref[0, hd, QK_NOPE + QK_ROPE:, rows] = jnp.zeros(
            (QK_PAD - QK_NOPE - QK_ROPE, rows.stop - rows.start), BF16)


def _q_tail_args(batch, seq, d, tile, shift, scale, cos2t, sin2t, g_pre, w_dq, g_q, w_nt, w_rt):
    qk = QK_PAD
    weights = (g_pre, w_dq, g_q, w_nt, w_rt)
    rope_spec = pl.BlockSpec((1, QK_ROPE, tile), lambda b, i: (b, 0, i))
    return dict(
        fn=_q_tail,
        operands=[shift, scale, cos2t, sin2t, *weights],
        in_specs=[_batch_vec_spec(d), _batch_vec_spec(d), rope_spec, rope_spec] +
                 [_const_spec(w.shape) for w in weights],
        out_specs=[pl.BlockSpec((1, MLA_HEADS, qk, tile), lambda b, i: (b, 0, 0, i))],
        out_shape=[jax.ShapeDtypeStruct((batch, MLA_HEADS, qk, seq), BF16)],
        name="ffn_q_tail",
    )


def _attn_kernel(qt_ref, k_ref, vt_ref, ot_ref, m_ref, l_ref, acc_ref, ahead_ref):
    group = qt_ref.shape[1]
    tq = qt_ref.shape[3]
    dv = vt_ref.shape[3]
    tk = ATTN_KEY_TILE
    qi = pl.program_id(2)
    neg = jnp.finfo(F32).min

    m_ref[...] = jnp.full_like(m_ref, -jnp.inf)
    l_ref[...] = jnp.zeros_like(l_ref)
    acc_ref[...] = jnp.zeros_like(acc_ref)

    def scores(hd, key_tile, q_start):
        rows = pl.ds(pl.multiple_of(key_tile * tk, tk), tk)
        return jnp.dot(k_ref[0, hd, rows, :], qt_ref[0, hd, :, q_start:],
                       preferred_element_type=F32)

    def update(hd, key_tile, q_start, s, masked):
        qcols = slice(q_start, tq)
        if masked:
            ki = lax.broadcasted_iota(jnp.int32, s.shape, 0)
            qj = lax.broadcasted_iota(jnp.int32, s.shape, 1)
            s = jnp.where(ki <= qj, s, neg)
        m_prev = m_ref[hd, :, qcols]
        m_new = jnp.maximum(m_prev, jnp.max(s, axis=0, keepdims=True))
        alpha = jnp.exp2(m_prev - m_new)
        p = jnp.exp2(s - m_new)
        l_ref[hd, :, qcols] = alpha * l_ref[hd, :, qcols] + jnp.sum(p, axis=0, keepdims=True)
        m_ref[hd, :, qcols] = m_new
        acc_ref[hd, :, qcols] = alpha * acc_ref[hd, :, qcols] + jnp.dot(
            vt_ref[0, hd, key_tile], p.astype(BF16), preferred_element_type=F32)

    tiles_per_span = tq // tk

    def span(first_tile, masked):
        steps = [(hd, first_tile + r, r * tk if masked else 0)
                 for r in range(tiles_per_span) for hd in range(group)]
        pending = [ahead_ref[i] for i in range(ATTN_LOOKAHEAD)]
        for idx, st in enumerate(steps):
            ahead = idx + ATTN_LOOKAHEAD
            if ahead < len(steps):
                pending.append(scores(*steps[ahead]))
            elif not masked:
                ahead_ref[ahead - len(steps)] = scores(
                    ahead - len(steps), first_tile + tiles_per_span, 0)
            update(*st, pending.pop(0), masked)

    for i in range(ATTN_LOOKAHEAD):
        ahead_ref[i] = scores(i, 0, 0)

    def full_span(j, carry):
        span(j * tiles_per_span, False)
        return carry

    lax.fori_loop(0, qi, full_span, 0)
    span(qi * tiles_per_span, True)
    for hd in range(group):
        ot_ref[0, hd * dv:(hd + 1) * dv, :] = (acc_ref[hd] / l_ref[hd]).astype(ot_ref.dtype)


def _attention(qt, k, vt):
    batch, heads, qk, seq = qt.shape
    _, _, n_key_tiles, dv, key_tile = vt.shape
    tile = ATTN_TILE
    group = ATTN_HEAD_GROUP
    return pl.pallas_call(
        _attn_kernel,
        grid=(batch, heads // group, seq // tile),
        in_specs=[
            pl.BlockSpec((1, group, qk, tile), lambda b, g, i: (b, g, 0, i)),
            pl.BlockSpec((1, group, seq, qk), lambda b, g, i: (b, g, 0, 0)),
            pl.BlockSpec((1, group, n_key_tiles, dv, key_tile),
                         lambda b, g, i: (b, g, 0, 0, 0)),
        ],
        out_specs=pl.BlockSpec((1, group * dv, tile), lambda b, g, i: (b, g, i)),
        out_shape=jax.ShapeDtypeStruct((batch, heads * dv, seq), BF16),
        scratch_shapes=[
            pltpu.VMEM((group, 1, tile), F32),
            pltpu.VMEM((group, 1, tile), F32),
            pltpu.VMEM((group, dv, tile), F32),
            pltpu.VMEM((ATTN_LOOKAHEAD, key_tile, tile), F32),
        ],
        compiler_params=_params("parallel", "parallel", "parallel"),
        name="mla_attention",
    )(qt, k, vt)


def kernel(x, c, positions, cond_w, cond_b, norm_g, ffn_w_gu, ffn_w_down, gla_w_in, gla_w_gate_up, gla_b_gate, gla_g_out, gla_w_out, kv_g_in, kv_cond_w, kv_cond_b, mla_w_kv_a, mla_g_kv, mla_w_kv_b, mla_w_dq, mla_g_q, mla_w_uq, mla_w_out):
    batch, seq, d = x.shape
    depth = cond_w.shape[0]
    n_a = gla_w_in.shape[0]
    d_ff = ffn_w_down.shape[2]
    kd = GLA_HEADS * (d // 2 // GLA_HEADS)
    vd = d

    n_mod = 3 * N_SUBLAYERS
    mods = _cond(c, cond_w, cond_b, cond_w.shape[2] // 4)
    mods = mods.reshape(depth, batch, n_mod, 1, d)
    kv_mods = _cond(c, kv_cond_w[None], kv_cond_b[None], kv_cond_w.shape[1] // 2)
    kv_mods = kv_mods.reshape(batch, 2, 1, d)

    def mod(layer, idx):
        return mods[layer, :, idx]

    def gain(layer, sub, which):
        return norm_g[layer, sub, which].reshape(1, d)

    ffn_w_gu_bf16 = ffn_w_gu.astype(BF16)
    ffn_w_down_bf16 = ffn_w_down.astype(BF16)

    def ffn(x, layer, half, sub, mixer_tail=None, tail=None):
        return _ffn(x, mod(layer, 3 * sub), mod(layer, 3 * sub + 1), mod(layer, 3 * sub + 2),
                    gain(layer, sub, 0), gain(layer, sub, 1),
                    ffn_w_gu_bf16, ffn_w_down_bf16, layer, half, 0.5, mixer_tail, tail)

    inv_freq = (ROPE_THETA ** (-jnp.arange(0, QK_ROPE, 2, dtype=F32) / QK_ROPE)
                ).reshape(QK_ROPE // 2, 1)
    pos = positions.reshape(batch, 1, seq)

    shared = None
    for layer in range(depth):
        mixer_tail = None
        if layer < n_a:
            i = layer
            x = ffn(x, layer, 0, 0)
            w_in = gla_w_in[i].astype(BF16)
            w_low = jnp.zeros((d, LANES), BF16).at[:, :GLA_GATE_RANK].set(
                w_in[:, 2 * kd + vd:2 * kd + vd + GLA_GATE_RANK])
            w_up = jnp.zeros((LANES, kd), BF16).at[:GLA_GATE_RANK].set(
                gla_w_gate_up[i].astype(BF16))
            x = _gla(x, mod(layer, 3), mod(layer, 4), mod(layer, 5),
                     gain(layer, 1, 0), gain(layer, 1, 1),
                     w_in[:, :kd], w_in[:, kd:2 * kd], w_in[:, 2 * kd:2 * kd + vd],
                     w_low, w_in[:, 2 * kd + vd + GLA_GATE_RANK:], w_up,
                     gla_b_gate[i].reshape(1, kd), gla_g_out[i].reshape(1, -1),
                     gla_w_out[i].astype(BF16))
        else:
            i = layer - n_a
            k_all, vt_all, cos2t, sin2t = shared
            w_uq = mla_w_uq[i].astype(BF16).reshape(-1, MLA_HEADS, QK_NOPE + QK_ROPE)
            q_tail = _q_tail_args(
                batch, seq, d, FFN_TILE, mod(layer, 3), mod(layer, 4), cos2t, sin2t,
                gain(layer, 1, 0), mla_w_dq[i].astype(BF16), mla_g_q[i].reshape(1, -1),
                w_uq[:, :, :QK_NOPE].reshape(-1, MLA_HEADS * QK_NOPE).T,
                w_uq[:, :, QK_NOPE:].reshape(-1, MLA_HEADS * QK_ROPE).T)
            x, qt = ffn(x, layer, 0, 0, tail=q_tail)
            at = _attention(qt, k_all, vt_all)
            mixer_tail = (at, mod(layer, 5), gain(layer, 1, 1), mla_w_out[i].astype(BF16))
        if layer == n_a - 1:
            w_kv_a = mla_w_kv_a.astype(BF16)
            w_kv_b = mla_w_kv_b.astype(BF16).reshape(KV_LORA, MLA_HEADS, QK_NOPE + V_HEAD)
            kv_tail = _kv_tail_args(
                batch, seq, d, FFN_TILE, kv_mods[:, 0], kv_mods[:, 1], pos, inv_freq,
                kv_g_in.reshape(1, d), w_kv_a[:, :KV_LORA], w_kv_a[:, KV_LORA:],
                mla_g_kv.reshape(1, KV_LORA),
                w_kv_b[:, :, :QK_NOPE].reshape(KV_LORA, MLA_HEADS * QK_NOPE),
                w_kv_b[:, :, QK_NOPE:].reshape(KV_LORA, MLA_HEADS * V_HEAD).T)
            x, *shared = ffn(x, layer, 1, 2, mixer_tail, kv_tail)
        else:
            x = ffn(x, layer, 1, 2, mixer_tail)
    return x
```

```python
import functools

import jax
import jax.numpy as jnp
from jax import lax
from jax.experimental import pallas as pl
from jax.experimental.pallas import tpu as pltpu

F32 = jnp.float32
BF16 = jnp.bfloat16

EPS = 1e-6
N_SUBLAYERS = 3
GLA_HEADS = 4
GLA_GATE_RANK = 16
GLA_TAU = 16.0
GLA_CHUNK = 64
GLA_BLOCK = 256
MLA_HEADS = 8
QK_NOPE = 128
QK_ROPE = 64
V_HEAD = 128
KV_LORA = 256
ROPE_THETA = 10000.0

LANES = 128
VMEM_LIMIT_BYTES = 56 * 1024 * 1024

TOKEN_TILE = 1024
GLA_PROJ_ROWS = 512
FFN_TILE = 1024
FFN_ROW_SPLIT = 4
FF_CHUNK = 256
ATTN_TILE = 512
ATTN_KEY_TILE = 256
ATTN_HEAD_GROUP = 4
ATTN_LOOKAHEAD = 2
LOG2E = 1.4426950408889634


def _params(*sem):
    return pltpu.CompilerParams(dimension_semantics=sem,
                                vmem_limit_bytes=VMEM_LIMIT_BYTES)


def _rms(x, g):
    return x * lax.rsqrt(jnp.mean(x * x, axis=-1, keepdims=True) + EPS) * g


def _silu(x):
    return x * jax.nn.sigmoid(x)


def _modulated_norm(x, g, scale, shift):
    return _rms(x, g) * (1.0 + scale) + shift


def _const_spec(shape):
    nd = len(shape)
    return pl.BlockSpec(shape, lambda *_: (0,) * nd)


def _tok_spec(tile, width):
    return pl.BlockSpec((1, tile, width), lambda b, i: (b, i, 0))


def _batch_vec_spec(width):
    return pl.BlockSpec((1, 1, width), lambda b, i: (b, 0, 0))


def _cond_kernel(c_ref, w_ref, b_ref, o_ref):
    c_act = _silu(c_ref[...])
    o_ref[0] = jnp.dot(c_act, w_ref[0], preferred_element_type=F32) + b_ref[0]


def _cond(c, w, b, col_tile):
    n_layers, d, n = w.shape
    batch = c.shape[0]
    return pl.pallas_call(
        _cond_kernel,
        grid=(n_layers, n // col_tile),
        in_specs=[
            pl.BlockSpec((batch, d), lambda l, j: (0, 0)),
            pl.BlockSpec((1, d, col_tile), lambda l, j: (l, 0, j)),
            pl.BlockSpec((1, 1, col_tile), lambda l, j: (l, 0, j)),
        ],
        out_specs=pl.BlockSpec((1, batch, col_tile), lambda l, j: (l, 0, j)),
        out_shape=jax.ShapeDtypeStruct((n_layers, batch, n), F32),
        compiler_params=_params("parallel", "parallel"),
        name="cond",
    )(c, w, b.reshape(n_layers, 1, n))


def _ffn_kernel(x_ref, shift_ref, scale_ref, gate_ref, gpre_ref, gpost_ref,
                wg_ref, wu_ref, wd_ref, *rest, res_weight, mixer_tail, tail_fn, n_tail_in):
    rest = list(rest)
    if mixer_tail:
        at_ref, mgate_ref, mgpost_ref, wout_ref = rest[:4]
        rest = rest[4:]
    tail_in = rest[:n_tail_in]
    o_ref = rest[n_tail_in]
    tail_out = rest[n_tail_in + 1:-1]
    act_ref = rest[-1]
    d_ff = wg_ref.shape[1]
    group = x_ref.shape[1] // FFN_ROW_SPLIT
    row_groups = [slice(r * group, (r + 1) * group) for r in range(FFN_ROW_SPLIT)]
    for rows in row_groups:
        x = x_ref[0, rows, :]
        if mixer_tail:
            y = lax.dot_general(at_ref[0, :, rows], wout_ref[...], (((0,), (0,)), ((), ())),
                                preferred_element_type=F32)
            x = x + mgate_ref[0] * _rms(y, mgpost_ref[...])
        o_ref[0, rows, :] = x
    for rows in row_groups:
        h = _modulated_norm(o_ref[0, rows, :], gpre_ref[...], scale_ref[0],
                            shift_ref[0]).astype(BF16)
        for j in range(d_ff // FF_CHUNK):
            cols = slice(j * FF_CHUNK, (j + 1) * FF_CHUNK)
            g = jnp.dot(h, wg_ref[:, cols], preferred_element_type=F32)
            u = jnp.dot(h, wu_ref[:, cols], preferred_element_type=F32)
            act_ref[rows, cols] = (_silu(g) * u).astype(BF16)
        y = jnp.dot(act_ref[rows, :], wd_ref[...], preferred_element_type=F32)
        o_ref[0, rows, :] = (o_ref[0, rows, :] +
                             (res_weight * gate_ref[0]) * _rms(y, gpost_ref[...]))
    if tail_fn is not None:
        tails = [tail_fn(o_ref[0, rows, :], rows, tail_in, tail_out) for rows in row_groups]
        while tails:
            tails = [t for t in tails if next(t, True) is None]


def _ffn(x, shift, scale, gate, g_pre, g_post, w_gu, w_d, layer, half, res_weight,
         mixer_tail=None, tail=None, tile=FFN_TILE):
    batch, seq, d = x.shape
    d_ff = w_d.shape[2]

    def resident(shape, col_block=0):
        return pl.BlockSpec((None, None) + shape, lambda b, i: (layer, half, 0, col_block),
                            pipeline_mode=pl.Buffered(1))

    operands = [x, shift, scale, gate, g_pre, g_post, w_gu, w_gu, w_d]
    in_specs = [
        _tok_spec(tile, d),
        _batch_vec_spec(d), _batch_vec_spec(d), _batch_vec_spec(d),
        _const_spec((1, d)), _const_spec((1, d)),
        resident((d, d_ff), 0), resident((d, d_ff), 1), resident((d_ff, d)),
    ]
    name = "ffn"
    if mixer_tail is not None:
        at, m_gate, m_gpost, w_out = mixer_tail
        operands += [at, m_gate, m_gpost, w_out]
        in_specs += [
            pl.BlockSpec((1, at.shape[1], tile), lambda b, i: (b, 0, i)),
            _batch_vec_spec(d), _const_spec((1, d)),
            pl.BlockSpec(w_out.shape, lambda b, i: (0, 0), pipeline_mode=pl.Buffered(1)),
        ]
        name = "ffn_mixer_tail"
    out_specs = [_tok_spec(tile, d)]
    out_shape = [jax.ShapeDtypeStruct(x.shape, x.dtype)]
    if tail is not None:
        operands += tail["operands"]
        in_specs += tail["in_specs"]
        out_specs += tail["out_specs"]
        out_shape += tail["out_shape"]
        name = tail["name"]
    outs = pl.pallas_call(
        functools.partial(_ffn_kernel, res_weight=res_weight,
                          mixer_tail=mixer_tail is not None,
                          tail_fn=None if tail is None else tail["fn"],
                          n_tail_in=0 if tail is None else len(tail["operands"])),
        grid=(batch, seq // tile),
        in_specs=in_specs,
        out_specs=out_specs,
        out_shape=out_shape,
        scratch_shapes=[pltpu.VMEM((tile, d_ff), BF16)],
        compiler_params=_params("parallel", "parallel"),
        name=name,
    )(*operands)
    return outs[0] if tail is None else outs


def _gla_kernel(x_ref, shift_ref, scale_ref, gate_ref, gpre_ref, gpost_ref,
                wq_ref, wk_ref, wv_ref, wlow_ref, wr_ref, wup_ref, bgate_ref,
                gout_ref, wout_ref, o_ref,
                state_ref, qd_ref, kd_ref, kut_ref, v_ref, dect_ref, upd_ref, mix_ref, r_ref):
    tile = x_ref.shape[1]
    n_chunks = tile // GLA_CHUNK
    n_blocks = tile // GLA_BLOCK
    chunks_per_block = GLA_BLOCK // GLA_CHUNK
    kd_all = wq_ref.shape[1]
    dk = kd_all // GLA_HEADS
    dv = wv_ref.shape[1] // GLA_HEADS

    @pl.when(pl.program_id(1) == 0)
    def _():
        state_ref[...] = jnp.zeros_like(state_ref)

    row = lax.broadcasted_iota(jnp.int32, (GLA_BLOCK, GLA_BLOCK), 0)
    col = lax.broadcasted_iota(jnp.int32, (GLA_BLOCK, GLA_BLOCK), 1)
    chunk_causal = (col <= row) & (col // GLA_CHUNK == row // GLA_CHUNK)
    tri = jnp.where(chunk_causal, 1.0, 0.0).astype(BF16)

    part_chunks = GLA_PROJ_ROWS // GLA_CHUNK
    n_parts = tile // GLA_PROJ_ROWS
    decays = [None] * n_parts

    def projection_part(part):
        rows = slice(part * GLA_PROJ_ROWS, (part + 1) * GLA_PROJ_ROWS)
        h = _modulated_norm(x_ref[0, rows, :], gpre_ref[...], scale_ref[0],
                            shift_ref[0]).astype(BF16)
        low = jnp.dot(h, wlow_ref[...], preferred_element_type=F32).astype(BF16)
        z = jnp.dot(low, wup_ref[...], preferred_element_type=F32) + bgate_ref[...]
        yield
        log_a = (jnp.minimum(z, 0.0) - jnp.log(1.0 + jnp.exp(-jnp.abs(z)))) / GLA_TAU
        q = jnp.dot(h, wq_ref[...], preferred_element_type=F32) * dk ** -0.5
        k = jnp.dot(h, wk_ref[...], preferred_element_type=F32)
        yield

        la_hi = log_a.astype(BF16)
        la_lo = (log_a - la_hi.astype(F32)).astype(BF16)
        b = jnp.concatenate([
            jnp.dot(tri, la_hi[i * GLA_BLOCK:(i + 1) * GLA_BLOCK], preferred_element_type=F32) +
            jnp.dot(tri, la_lo[i * GLA_BLOCK:(i + 1) * GLA_BLOCK], preferred_element_type=F32)
            for i in range(GLA_PROJ_ROWS // GLA_BLOCK)], axis=0)
        v_ref[rows, :] = jnp.dot(h, wv_ref[...], preferred_element_type=F32).astype(BF16)
        r_ref[rows, :] = jnp.dot(h, wr_ref[...], preferred_element_type=F32)
        yield
        b3 = b.reshape(part_chunks, GLA_CHUNK, kd_all)
        b_last = b3[:, GLA_CHUNK - 1:, :]
        qd_ref[rows, :] = (q * jnp.exp(b)).astype(BF16)
        kd_ref[rows, :] = (k * jnp.exp(-b)).astype(BF16)
        k_upd = (k.reshape(part_chunks, GLA_CHUNK, kd_all) *
                 jnp.exp(b_last - b3)).reshape(GLA_PROJ_ROWS, kd_all)
        kut_ref[:, rows] = k_upd.T.astype(BF16)
        decays[part] = jnp.exp(b_last).reshape(part_chunks, kd_all)

    parts = [projection_part(part) for part in range(n_parts)]
    live = set(range(n_parts))
    turn = 0
    while live:
        for part in sorted(live):
            if turn >= part and next(parts[part], True) is not None:
                live.discard(part)
        turn += 1
    dec = jnp.concatenate(decays + [jnp.zeros((LANES - n_chunks, kd_all), F32)], axis=0)
    dect_ref[...] = dec.T

    lane_chunk = lax.broadcasted_iota(jnp.int32, (dk, GLA_BLOCK), 1) // GLA_CHUNK
    nt = (((1,), (1,)), ((), ()))

    steps = [(hd, blk) for blk in range(n_blocks) for hd in range(GLA_HEADS)]

    def attention_block(hd, blk):
        rows = slice(blk * GLA_BLOCK, (blk + 1) * GLA_BLOCK)
        kcols = slice(hd * dk, (hd + 1) * dk)
        return lax.dot_general(qd_ref[rows, kcols], kd_ref[rows, kcols], nt,
                               preferred_element_type=F32)

    def value_block(hd, blk, att):
        rows = slice(blk * GLA_BLOCK, (blk + 1) * GLA_BLOCK)
        vcols = slice(hd * dv, (hd + 1) * dv)
        kut = kut_ref[hd * dk:(hd + 1) * dk, rows]
        lhs = [jnp.where(chunk_causal, att, 0.0).astype(BF16)]
        for n in range(chunks_per_block):
            lhs.append(jnp.where(lane_chunk == n, kut, jnp.zeros_like(kut)))
        res = jnp.dot(jnp.concatenate(lhs, axis=0), v_ref[rows, vcols],
                      preferred_element_type=F32)
        mix_ref[rows, vcols] = res[:GLA_BLOCK]
        for n in range(chunks_per_block):
            upd_ref[hd, blk * chunks_per_block + n] = res[GLA_BLOCK + n * dk:
                                                         GLA_BLOCK + (n + 1) * dk]

    pending = attention_block(*steps[0])
    for idx, st in enumerate(steps):
        nxt = attention_block(*steps[idx + 1]) if idx + 1 < len(steps) else None
        value_block(*st, pending)
        pending = nxt

    for n in range(n_chunks):
        rows = slice(n * GLA_CHUNK, (n + 1) * GLA_CHUNK)
        for hd in range(GLA_HEADS):
            kcols = slice(hd * dk, (hd + 1) * dk)
            vcols = slice(hd * dv, (hd + 1) * dv)
            state = state_ref[hd]
            mix_ref[rows, vcols] += jnp.dot(qd_ref[rows, kcols], state.astype(BF16),
                                            preferred_element_type=F32)
            state_ref[hd] = dect_ref[kcols, n:n + 1] * state + upd_ref[hd, n]

    for hd in range(GLA_HEADS):
        vcols = slice(hd * dv, (hd + 1) * dv)
        mix_ref[:, vcols] = _rms(mix_ref[:, vcols], gout_ref[...])
    y = (mix_ref[...] * _silu(r_ref[...])).astype(BF16)
    y = jnp.dot(y, wout_ref[...], preferred_element_type=F32)
    o_ref[0] = x_ref[0] + gate_ref[0] * _rms(y, gpost_ref[...])


def _gla(x, shift, scale, gate, g_pre, g_post, w_q, w_k, w_v, w_low, w_r, w_up,
         b_gate, g_out, w_out):
    batch, seq, d = x.shape
    tile = TOKEN_TILE
    kd_all = w_q.shape[1]
    vd_all = w_v.shape[1]
    dk = kd_all // GLA_HEADS
    dv = vd_all // GLA_HEADS
    weights = (w_q, w_k, w_v, w_low, w_r, w_up, b_gate, g_out, w_out)
    return pl.pallas_call(
        _gla_kernel,
        grid=(batch, seq // tile),
        in_specs=[
            _tok_spec(tile, d),
            _batch_vec_spec(d), _batch_vec_spec(d), _batch_vec_spec(d),
            _const_spec((1, d)), _const_spec((1, d)),
        ] + [pl.BlockSpec(w.shape, lambda b, i: (0, 0), pipeline_mode=pl.Buffered(1))
             for w in weights],
        out_specs=_tok_spec(tile, d),
        out_shape=jax.ShapeDtypeStruct(x.shape, x.dtype),
        scratch_shapes=[
            pltpu.VMEM((GLA_HEADS, dk, dv), F32),
            pltpu.VMEM((tile, kd_all), BF16),
            pltpu.VMEM((tile, kd_all), BF16),
            pltpu.VMEM((kd_all, tile), BF16),
            pltpu.VMEM((tile, vd_all), BF16),
            pltpu.VMEM((kd_all, LANES), F32),
            pltpu.VMEM((GLA_HEADS, tile // GLA_CHUNK, dk, dv), F32),
            pltpu.VMEM((tile, vd_all), F32),
            pltpu.VMEM((tile, vd_all), F32),
        ],
        compiler_params=_params("parallel", "arbitrary"),
        name="gla",
    )(x, shift, scale, gate, g_pre, g_post, *weights)


def _swap_halves(x):
    half = x.shape[-1] // 2
    return jnp.concatenate([x[:, half:], x[:, :half]], axis=-1)


def _rope(x, cos2, sin2):
    return x * cos2 + _swap_halves(x) * sin2


def _kv_tail(x, rows, in_refs, out_refs):
    (shift_ref, scale_ref, pos_ref, freq_ref, gin_ref,
     wc_ref, wpe_ref, gkv_ref, wbk_ref, wbvt_ref) = in_refs
    k_ref, vt_ref, cos_ref, sin_ref = out_refs
    h = _modulated_norm(x, gin_ref[...], scale_ref[0], shift_ref[0]).astype(BF16)
    c_kv = jnp.dot(h, wc_ref[...], preferred_element_type=F32)
    k_pe = jnp.dot(h, wpe_ref[...], preferred_element_type=F32)
    yield
    c_kv = _rms(c_kv, gkv_ref[...]).astype(BF16)
    k_nope = jnp.dot(c_kv, wbk_ref[...], preferred_element_type=F32)
    vt = lax.dot_general(wbvt_ref[...], c_kv, (((1,), (1,)), ((), ())),
                         preferred_element_type=F32).astype(BF16)
    yield
    n_rows = rows.stop - rows.start
    first_key_tile = rows.start // ATTN_KEY_TILE
    for hd in range(MLA_HEADS):
        for r in range(n_rows // ATTN_KEY_TILE):
            vt_ref[0, hd, first_key_tile + r] = vt[hd * V_HEAD:(hd + 1) * V_HEAD,
                                                   r * ATTN_KEY_TILE:(r + 1) * ATTN_KEY_TILE]

    ang = freq_ref[...] * pos_ref[0, :, rows].astype(F32)
    cos = jnp.cos(ang)
    sin = jnp.sin(ang)
    cos2t = jnp.concatenate([cos, cos], axis=0)
    sin2t = jnp.concatenate([-sin, sin], axis=0)
    cos_ref[0, :, rows] = cos2t
    sin_ref[0, :, rows] = sin2t
    k_rope = _rope(k_pe, cos2t.T, sin2t.T).astype(BF16)
    for hd in range(MLA_HEADS):
        k_ref[0, hd, rows, :QK_NOPE] = k_nope[:, hd * QK_NOPE:(hd + 1) * QK_NOPE].astype(BF16)
        k_ref[0, hd, rows, QK_NOPE:] = k_rope


def _kv_tail_args(batch, seq, d, tile, shift, scale, pos, inv_freq, g_in, w_c, w_pe, g_kv,
                  w_bk, w_bvt):
    qk = QK_NOPE + QK_ROPE
    weights = (g_in, w_c, w_pe, g_kv, w_bk, w_bvt)

    def transposed_spec(rows):
        return pl.BlockSpec((1, rows, tile), lambda b, i: (b, 0, i))

    return dict(
        fn=_kv_tail,
        operands=[shift, scale, pos, inv_freq, *weights],
        in_specs=[_batch_vec_spec(d), _batch_vec_spec(d), transposed_spec(1),
                  _const_spec(inv_freq.shape)] + [_const_spec(w.shape) for w in weights],
        out_specs=[
            pl.BlockSpec((1, MLA_HEADS, tile, qk), lambda b, i: (b, 0, i, 0)),
            pl.BlockSpec((1, MLA_HEADS, tile // ATTN_KEY_TILE, V_HEAD, ATTN_KEY_TILE),
                         lambda b, i: (b, 0, i, 0, 0)),
            transposed_spec(QK_ROPE), transposed_spec(QK_ROPE),
        ],
        out_shape=[
            jax.ShapeDtypeStruct((batch, MLA_HEADS, seq, qk), BF16),
            jax.ShapeDtypeStruct((batch, MLA_HEADS, seq // ATTN_KEY_TILE, V_HEAD, ATTN_KEY_TILE),
                                 BF16),
            jax.ShapeDtypeStruct((batch, QK_ROPE, seq), F32),
            jax.ShapeDtypeStruct((batch, QK_ROPE, seq), F32),
        ],
        name="ffn_kv_tail",
    )


def _swap_halves_rows(x):
    half = x.shape[0] // 2
    return jnp.concatenate([x[half:], x[:half]], axis=0)


def _q_tail(x, rows, in_refs, out_refs):
    (shift_ref, scale_ref, cos_ref, sin_ref, gpre_ref,
     wdq_ref, gq_ref, wnt_ref, wrt_ref) = in_refs
    (q_ref,) = out_refs
    h = _modulated_norm(x, gpre_ref[...], scale_ref[0], shift_ref[0]).astype(BF16)
    c_q = jnp.dot(h, wdq_ref[...], preferred_element_type=F32)
    yield
    c_q = _rms(c_q, gq_ref[...]).astype(BF16)
    q_scale = (QK_NOPE + QK_ROPE) ** -0.5 * LOG2E
    nt = (((1,), (1,)), ((), ()))
    q_nope = lax.dot_general(wnt_ref[...], c_q, nt, preferred_element_type=F32) * q_scale
    q_rope = lax.dot_general(wrt_ref[...], c_q, nt, preferred_element_type=F32) * q_scale
    yield
    cos2 = cos_ref[0, :, rows]
    sin2 = sin_ref[0, :, rows]
    for hd in range(MLA_HEADS):
        q_ref[0, hd, :QK_NOPE, rows] = q_nope[hd * QK_NOPE:(hd + 1) * QK_NOPE].astype(BF16)
        qr = q_rope[hd * QK_ROPE:(hd + 1) * QK_ROPE]
        q_ref[0, hd, QK_NOPE:, rows] = (qr * cos2 + _swap_halves_rows(qr) * sin2).astype(BF16)


def _q_tail_args(batch, seq, d, tile, shift, scale, cos2t, sin2t, g_pre, w_dq, g_q, w_nt, w_rt):
    qk = QK_NOPE + QK_ROPE
    weights = (g_pre, w_dq, g_q, w_nt, w_rt)
    rope_spec = pl.BlockSpec((1, QK_ROPE, tile), lambda b, i: (b, 0, i))
    return dict(
        fn=_q_tail,
        operands=[shift, scale, cos2t, sin2t, *weights],
        in_specs=[_batch_vec_spec(d), _batch_vec_spec(d), rope_spec, rope_spec] +
                 [_const_spec(w.shape) for w in weights],
        out_specs=[pl.BlockSpec((1, MLA_HEADS, qk, tile), lambda b, i: (b, 0, 0, i))],
        out_shape=[jax.ShapeDtypeStruct((batch, MLA_HEADS, qk, seq), BF16)],
        name="ffn_q_tail",
    )


def _attn_kernel(qt_ref, k_ref, vt_ref, ot_ref, m_ref, l_ref, acc_ref, ahead_ref):
    group = qt_ref.shape[1]
    tq = qt_ref.shape[3]
    dv = vt_ref.shape[3]
    tk = ATTN_KEY_TILE
    qi = pl.program_id(2)
    neg = jnp.finfo(F32).min

    m_ref[...] = jnp.full_like(m_ref, -jnp.inf)
    l_ref[...] = jnp.zeros_like(l_ref)
    acc_ref[...] = jnp.zeros_like(acc_ref)

    def scores(hd, key_tile, q_start):
        rows = pl.ds(pl.multiple_of(key_tile * tk, tk), tk)
        return jnp.dot(k_ref[0, hd, rows, :], qt_ref[0, hd, :, q_start:],
                       preferred_element_type=F32)

    def update(hd, key_tile, q_start, s, masked):
        qcols = slice(q_start, tq)
        if masked:
            ki = lax.broadcasted_iota(jnp.int32, s.shape, 0)
            qj = lax.broadcasted_iota(jnp.int32, s.shape, 1)
            s = jnp.where(ki <= qj, s, neg)
        m_prev = m_ref[hd, :, qcols]
        m_new = jnp.maximum(m_prev, jnp.max(s, axis=0, keepdims=True))
        alpha = jnp.exp2(m_prev - m_new)
        p = jnp.exp2(s - m_new)
        l_ref[hd, :, qcols] = alpha * l_ref[hd, :, qcols] + jnp.sum(p, axis=0, keepdims=True)
        m_ref[hd, :, qcols] = m_new
        acc_ref[hd, :, qcols] = alpha * acc_ref[hd, :, qcols] + jnp.dot(
            vt_ref[0, hd, key_tile], p.astype(BF16), preferred_element_type=F32)

    tiles_per_span = tq // tk

    def span(first_tile, masked):
        steps = [(hd, first_tile + r, r * tk if masked else 0)
                 for r in range(tiles_per_span) for hd in range(group)]
        pending = [ahead_ref[i] for i in range(ATTN_LOOKAHEAD)]
        for idx, st in enumerate(steps):
            ahead = idx + ATTN_LOOKAHEAD
            if ahead < len(steps):
                pending.append(scores(*steps[ahead]))
            elif not masked:
                ahead_ref[ahead - len(steps)] = scores(
                    ahead - len(steps), first_tile + tiles_per_span, 0)
            update(*st, pending.pop(0), masked)

    for i in range(ATTN_LOOKAHEAD):
        ahead_ref[i] = scores(i, 0, 0)

    def full_span(j, carry):
        span(j * tiles_per_span, False)
        return carry

    lax.fori_loop(0, qi, full_span, 0)
    span(qi * tiles_per_span, True)
    for hd in range(group):
        ot_ref[0, hd * dv:(hd + 1) * dv, :] = (acc_ref[hd] / l_ref[hd]).astype(ot_ref.dtype)


def _attention(qt, k, vt):
    batch, heads, qk, seq = qt.shape
    _, _, n_key_tiles, dv, key_tile = vt.shape
    tile = ATTN_TILE
    group = ATTN_HEAD_GROUP
    return pl.pallas_call(
        _attn_kernel,
        grid=(batch, heads // group, seq // tile),
        in_specs=[
            pl.BlockSpec((1, group, qk, tile), lambda b, g, i: (b, g, 0, i)),
            pl.BlockSpec((1, group, seq, qk), lambda b, g, i: (b, g, 0, 0)),
            pl.BlockSpec((1, group, n_key_tiles, dv, key_tile),
                         lambda b, g, i: (b, g, 0, 0, 0)),
        ],
        out_specs=pl.BlockSpec((1, group * dv, tile), lambda b, g, i: (b, g, i)),
        out_shape=jax.ShapeDtypeStruct((batch, heads * dv, seq), BF16),
        scratch_shapes=[
            pltpu.VMEM((group, 1, tile), F32),
            pltpu.VMEM((group, 1, tile), F32),
            pltpu.VMEM((group, dv, tile), F32),
            pltpu.VMEM((ATTN_LOOKAHEAD, key_tile, tile), F32),
        ],
        compiler_params=_params("parallel", "parallel", "parallel"),
        name="mla_attention",
    )(qt, k, vt)


def kernel(x, c, positions, cond_w, cond_b, norm_g, ffn_w_gu, ffn_w_down, gla_w_in, gla_w_gate_up, gla_b_gate, gla_g_out, gla_w_out, kv_g_in, kv_cond_w, kv_cond_b, mla_w_kv_a, mla_g_kv, mla_w_kv_b, mla_w_dq, mla_g_q, mla_w_uq, mla_w_out):
    batch, seq, d = x.shape
    depth = cond_w.shape[0]
    n_a = gla_w_in.shape[0]
    d_ff = ffn_w_down.shape[2]
    kd = GLA_HEADS * (d // 2 // GLA_HEADS)
    vd = d

    n_mod = 3 * N_SUBLAYERS
    mods = _cond(c, cond_w, cond_b, cond_w.shape[2] // 4)
    mods = mods.reshape(depth, batch, n_mod, 1, d)
    kv_mods = _cond(c, kv_cond_w[None], kv_cond_b[None], kv_cond_w.shape[1] // 2)
    kv_mods = kv_mods.reshape(batch, 2, 1, d)

    def mod(layer, idx):
        return mods[layer, :, idx]

    def gain(layer, sub, which):
        return norm_g[layer, sub, which].reshape(1, d)

    ffn_w_gu_bf16 = ffn_w_gu.astype(BF16)
    ffn_w_down_bf16 = ffn_w_down.astype(BF16)

    def ffn(x, layer, half, sub, mixer_tail=None, tail=None):
        return _ffn(x, mod(layer, 3 * sub), mod(layer, 3 * sub + 1), mod(layer, 3 * sub + 2),
                    gain(layer, sub, 0), gain(layer, sub, 1),
                    ffn_w_gu_bf16, ffn_w_down_bf16, layer, half, 0.5, mixer_tail, tail)

    inv_freq = (ROPE_THETA ** (-jnp.arange(0, QK_ROPE, 2, dtype=F32) / QK_ROPE)
                ).reshape(QK_ROPE // 2, 1)
    pos = positions.reshape(batch, 1, seq)

    shared = None
    for layer in range(depth):
        mixer_tail = None
        if layer < n_a:
            i = layer
            x = ffn(x, layer, 0, 0)
            w_in = gla_w_in[i].astype(BF16)
            w_low = jnp.zeros((d, LANES), BF16).at[:, :GLA_GATE_RANK].set(
                w_in[:, 2 * kd + vd:2 * kd + vd + GLA_GATE_RANK])
            w_up = jnp.zeros((LANES, kd), BF16).at[:GLA_GATE_RANK].set(
                gla_w_gate_up[i].astype(BF16))
            x = _gla(x, mod(layer, 3), mod(layer, 4), mod(layer, 5),
                     gain(layer, 1, 0), gain(layer, 1, 1),
                     w_in[:, :kd], w_in[:, kd:2 * kd], w_in[:, 2 * kd:2 * kd + vd],
                     w_low, w_in[:, 2 * kd + vd + GLA_GATE_RANK:], w_up,
                     gla_b_gate[i].reshape(1, kd), gla_g_out[i].reshape(1, -1),
                     gla_w_out[i].astype(BF16))
        else:
            i = layer - n_a
            k_all, vt_all, cos2t, sin2t = shared
            w_uq = mla_w_uq[i].astype(BF16).reshape(-1, MLA_HEADS, QK_NOPE + QK_ROPE)
            q_tail = _q_tail_args(
                batch, seq, d, FFN_TILE, mod(layer, 3), mod(layer, 4), cos2t, sin2t,
                gain(layer, 1, 0), mla_w_dq[i].astype(BF16), mla_g_q[i].reshape(1, -1),
                w_uq[:, :, :QK_NOPE].reshape(-1, MLA_HEADS * QK_NOPE).T,
                w_uq[:, :, QK_NOPE:].reshape(-1, MLA_HEADS * QK_ROPE).T)
            x, qt = ffn(x, layer, 0, 0, tail=q_tail)
            at = _attention(qt, k_all, vt_all)
            mixer_tail = (at, mod(layer, 5), gain(layer, 1, 1), mla_w_out[i].astype(BF16))
        if layer == n_a - 1:
            w_kv_a = mla_w_kv_a.astype(BF16)
            w_kv_b = mla_w_kv_b.astype(BF16).reshape(KV_LORA, MLA_HEADS, QK_NOPE + V_HEAD)
            kv_tail = _kv_tail_args(
                batch, seq, d, FFN_TILE, kv_mods[:, 0], kv_mods[:, 1], pos, inv_freq,
                kv_g_in.reshape(1, d), w_kv_a[:, :KV_LORA], w_kv_a[:, KV_LORA:],
                mla_g_kv.reshape(1, KV_LORA),
                w_kv_b[:, :, :QK_NOPE].reshape(KV_LORA, MLA_HEADS * QK_NOPE),
                w_kv_b[:, :, QK_NOPE:].reshape(KV_LORA, MLA_HEADS * V_HEAD).T)
            x, *shared = ffn(x, layer, 1, 2, mixer_tail, kv_tail)
        else:
            x = ffn(x, layer, 1, 2, mixer_tail)
    return x
```
